```python
import jax, jax.numpy as jnp
from jax import lax
import numpy as np

D_MODEL = 2048
BATCH = 4
SEQ = 4096
DEPTH = 1

CHUNK = 64
N_PREV_CHUNKS = 8
BAND_CHUNKS = N_PREV_CHUNKS + 1
BAND_LEN = BAND_CHUNKS * CHUNK
ATT_HEADS = 8
ATT_HEAD_DIM = 128
ATT_WIDTH = ATT_HEADS * ATT_HEAD_DIM
MAX_REL_DIST = 128
N_REL = 2 * MAX_REL_DIST + 1
RET_HEADS = 8
RET_KEY_DIM = 128
RET_VAL_DIM = 128
RET_QK_WIDTH = RET_HEADS * RET_KEY_DIM
RET_V_WIDTH = RET_HEADS * RET_VAL_DIM
ROPE_BASE = 10000.0
D_FF = 5632
EPS = 1e-6
IN_SPLIT_WIDTHS = (ATT_WIDTH, ATT_WIDTH, ATT_WIDTH, RET_QK_WIDTH, RET_QK_WIDTH,
                   RET_V_WIDTH, RET_V_WIDTH, D_MODEL, D_MODEL)
IN_WIDTH = sum(IN_SPLIT_WIDTHS)
IN_SPLIT_POINTS = tuple(int(v) for v in np.cumsum(IN_SPLIT_WIDTHS)[:-1])

kernel_name = "macaron_gated_chunkattn_retention_block"


def rmsnorm(x, g):
    xf = x.astype(jnp.float32)
    y = xf * lax.rsqrt(jnp.mean(xf * xf, axis=-1, keepdims=True) + EPS)
    return (y * g.astype(jnp.float32)).astype(x.dtype)


def swiglu_ffn(x, w_gate, w_up, w_down):
    return (jax.nn.silu(x @ w_gate) * (x @ w_up)) @ w_down


def rotary(x):
    S, d = x.shape[1], x.shape[-1]
    inv = 1.0 / (ROPE_BASE ** (jnp.arange(0, d, 2, dtype=jnp.float32) / d))
    ang = jnp.arange(S, dtype=jnp.float32)[:, None] * inv[None, :]
    cos = jnp.cos(ang)[None, :, None, :].astype(x.dtype)
    sin = jnp.sin(ang)[None, :, None, :].astype(x.dtype)
    x1, x2 = x[..., : d // 2], x[..., d // 2:]
    return jnp.concatenate([x1 * cos - x2 * sin, x1 * sin + x2 * cos], axis=-1)


def chunked_band_attention(q, k, v, rel_bias_table):
    B, S, H, hd = q.shape
    nc = S // CHUNK
    qc = q.reshape(B, nc, CHUNK, H, hd)
    kc = k.reshape(B, nc, CHUNK, H, hd)
    vc = v.reshape(B, nc, CHUNK, H, hd)
    pad = ((0, 0), (N_PREV_CHUNKS, 0), (0, 0), (0, 0), (0, 0))
    kp = jnp.pad(kc, pad)
    vp = jnp.pad(vc, pad)
    k_band = jnp.concatenate([kp[:, j:j + nc] for j in range(BAND_CHUNKS)], axis=2)
    v_band = jnp.concatenate([vp[:, j:j + nc] for j in range(BAND_CHUNKS)], axis=2)
    scores = jnp.einsum('bcqhd,bckhd->bhcqk', qc, k_band,
                        preferred_element_type=jnp.float32) * (hd ** -0.5)
    q_pos = jnp.arange(CHUNK) + N_PREV_CHUNKS * CHUNK
    k_pos = jnp.arange(BAND_LEN)
    rel_idx = jnp.clip(q_pos[:, None] - k_pos[None, :], -MAX_REL_DIST, MAX_REL_DIST) + MAX_REL_DIST
    bias = rel_bias_table.astype(jnp.float32)[:, rel_idx]
    band_chunk = k_pos // CHUNK
    valid = (jnp.arange(nc)[:, None] - N_PREV_CHUNKS + band_chunk[None, :]) >= 0
    scores = scores + bias[None, :, None, :, :]
    scores = jnp.where(valid[None, None, :, None, :], scores, jnp.float32(-1e30))
    p = jax.nn.softmax(scores, axis=-1).astype(v.dtype)
    out = jnp.einsum('bhcqk,bckhd->bcqhd', p, v_band)
    return out.reshape(B, S, H * hd)


def chunkwise_retention(q, k, v):
    B, S, H, dk = q.shape
    dv = v.shape[-1]
    nc = S // CHUNK
    f32 = jnp.float32
    gamma = 1.0 - 2.0 ** (-5.0 - jnp.arange(H, dtype=f32))
    log_g = jnp.log(gamma)
    idx = jnp.arange(CHUNK, dtype=f32)
    intra = jnp.exp(log_g[:, None, None] * jnp.abs(idx[:, None] - idx[None, :]))
    q_decay = jnp.exp(log_g[:, None] * (idx[None, :] + 1.0))
    k_decay = jnp.exp(log_g[:, None] * (CHUNK - 1.0 - idx[None, :]))
    chunk_decay = jnp.exp(log_g * CHUNK)

    def to_chunks(t):
        return t.astype(f32).reshape(B, nc, CHUNK, H, t.shape[-1]).transpose(1, 0, 3, 2, 4)

    qc = to_chunks(q) * (dk ** -0.5)
    kc = to_chunks(k)
    vc = to_chunks(v)

    def step(state, inp):
        qi, ki, vi = inp
        s = jnp.einsum('bhqd,bhkd->bhqk', qi, ki) * intra[None]
        inner = jnp.einsum('bhqk,bhkv->bhqv', s, vi)
        cross = jnp.einsum('bhqd,bhdv->bhqv', qi, state) * q_decay[None, :, :, None]
        new_state = state * chunk_decay[None, :, None, None] + jnp.einsum(
            'bhkd,bhkv->bhdv', ki * k_decay[None, :, :, None], vi)
        return new_state, inner + cross

    state0 = jnp.zeros((B, H, dk, dv), f32)
    _, out = lax.scan(step, state0, (qc, kc, vc))
    out = out.transpose(1, 0, 3, 2, 4).reshape(B, S, H, dv)
    out = out * lax.rsqrt(jnp.mean(out * out, axis=-1, keepdims=True) + EPS)
    return out.astype(v.dtype)


def setup_inputs(seed: int = 0) -> dict:
    key = jax.random.key(seed)
    ks = jax.random.split(key, 16)
    f32 = jnp.float32

    def w(k, shape, fan_in):
        return jax.random.normal(k, shape, f32) * (fan_in ** -0.5)

    def gain(k):
        return 1.0 + 0.1 * jax.random.normal(k, (DEPTH, D_MODEL), f32)

    return {
        "x": jax.random.normal(ks[0], (BATCH, SEQ, D_MODEL), f32),
        "norm_ffn1_g": gain(ks[1]),
        "ffn1_w_gate": w(ks[2], (DEPTH, D_MODEL, D_FF), D_MODEL),
        "ffn1_w_up": w(ks[3], (DEPTH, D_MODEL, D_FF), D_MODEL),
        "ffn1_w_down": w(ks[4], (DEPTH, D_FF, D_MODEL), D_FF),
        "norm_mix_g": gain(ks[5]),
        "w_in": w(ks[6], (DEPTH, D_MODEL, IN_WIDTH), D_MODEL),
        "rel_bias": 0.5 * jax.random.normal(ks[7], (DEPTH, ATT_HEADS, N_REL), f32),
        "w_out_att": w(ks[8], (DEPTH, ATT_WIDTH, D_MODEL), ATT_WIDTH),
        "w_out_ret": w(ks[9], (DEPTH, RET_V_WIDTH, D_MODEL), RET_V_WIDTH),
        "w_out": w(ks[10], (DEPTH, D_MODEL, D_MODEL), D_MODEL),
        "norm_ffn2_g": gain(ks[11]),
        "ffn2_w_gate": w(ks[12], (DEPTH, D_MODEL, D_FF), D_MODEL),
        "ffn2_w_up": w(ks[13], (DEPTH, D_MODEL, D_FF), D_MODEL),
        "ffn2_w_down": w(ks[14], (DEPTH, D_FF, D_MODEL), D_FF),
        "norm_final_g": 1.0 + 0.1 * jax.random.normal(ks[15], (D_MODEL,), f32),
    }


def reference(x, norm_ffn1_g, ffn1_w_gate, ffn1_w_up, ffn1_w_down, norm_mix_g, w_in, rel_bias,
              w_out_att, w_out_ret, w_out, norm_ffn2_g, ffn2_w_gate, ffn2_w_up, ffn2_w_down,
              norm_final_g):
    B, S, _ = x.shape
    for l in range(DEPTH):
        x = x + 0.5 * swiglu_ffn(rmsnorm(x, norm_ffn1_g[l]), ffn1_w_gate[l], ffn1_w_up[l], ffn1_w_down[l])

        h = rmsnorm(x, norm_mix_g[l])
        proj = h @ w_in[l]
        q_a, k_a, v_a, q_r, k_r, v_r, g_r, gate_a, gate_r = jnp.split(proj, IN_SPLIT_POINTS, axis=-1)

        att = chunked_band_attention(
            q_a.reshape(B, S, ATT_HEADS, ATT_HEAD_DIM),
            k_a.reshape(B, S, ATT_HEADS, ATT_HEAD_DIM),
            v_a.reshape(B, S, ATT_HEADS, ATT_HEAD_DIM),
            rel_bias[l])
        branch_a = att @ w_out_att[l]

        ret = chunkwise_retention(
            rotary(q_r.reshape(B, S, RET_HEADS, RET_KEY_DIM)),
            rotary(k_r.reshape(B, S, RET_HEADS, RET_KEY_DIM)),
            v_r.reshape(B, S, RET_HEADS, RET_VAL_DIM))
        ret = jax.nn.silu(g_r) * ret.reshape(B, S, RET_V_WIDTH)
        branch_r = ret @ w_out_ret[l]

        merged = jax.nn.sigmoid(gate_a) * branch_a + jax.nn.sigmoid(gate_r) * branch_r
        x = x + merged @ w_out[l]

        x = x + 0.5 * swiglu_ffn(rmsnorm(x, norm_ffn2_g[l]), ffn2_w_gate[l], ffn2_w_up[l], ffn2_w_down[l])
    return rmsnorm(x, norm_final_g)
```

```python
import functools
import math

import numpy as np
import jax
import jax.numpy as jnp
from jax import lax
from jax.experimental import pallas as pl
from jax.experimental.pallas import tpu as pltpu

F32 = jnp.float32
BF16 = jnp.bfloat16

CHUNK = 64
N_PREV_CHUNKS = 8
HEADS = 8
HEAD_DIM = 128
WIDTH = HEADS * HEAD_DIM
MAX_REL_DIST = 128
ROPE_BASE = 10000.0
EPS = 1e-6
NEG = -1e30

TQ = 256
N_KBLK = 3
MIB = 1024 * 1024


def _rmsnorm(x, g):
    ms = jnp.mean(x * x, axis=-1, keepdims=True)
    return x * lax.rsqrt(ms + EPS) * g


def _silu(x):
    return x * jax.nn.sigmoid(x)


def _ffn_kernel(*refs, n_f, final_norm):
    if final_norm:
        x_ref, g_ref, wg_ref, wu_ref, wd_ref, gf_ref, o_ref, h_ref = refs
    else:
        x_ref, g_ref, wg_ref, wu_ref, wd_ref, o_ref, h_ref = refs
    j = pl.program_id(1)

    @pl.when(j == 0)
    def _():
        h_ref[...] = _rmsnorm(x_ref[...], g_ref[...]).astype(BF16)

    h = h_ref[...]
    gate = jnp.dot(h, wg_ref[...], preferred_element_type=F32)
    up = jnp.dot(h, wu_ref[...], preferred_element_type=F32)
    act = (_silu(gate) * up).astype(BF16)
    part = jnp.dot(act, wd_ref[...], preferred_element_type=F32)

    @pl.when(j == 0)
    def _():
        o_ref[...] = part

    @pl.when(j > 0)
    def _():
        o_ref[...] += part

    @pl.when(j == n_f - 1)
    def _():
        y = x_ref[...] + 0.5 * o_ref[...]
        if final_norm:
            y = _rmsnorm(y, gf_ref[...])
        o_ref[...] = y


def _ffn(x, g, wg, wu, wd, g_final=None, *, tm=512, tf=512):
    t, d = x.shape
    f = wg.shape[1]
    n_f = f // tf
    final_norm = g_final is not None
    in_specs = [
        pl.BlockSpec((tm, d), lambda i, j: (i, 0)),
        pl.BlockSpec((1, d), lambda i, j: (0, 0)),
        pl.BlockSpec((d, tf), lambda i, j: (0, j)),
        pl.BlockSpec((d, tf), lambda i, j: (0, j)),
        pl.BlockSpec((tf, d), lambda i, j: (j, 0)),
    ]
    args = [x, g, wg, wu, wd]
    if final_norm:
        in_specs.append(pl.BlockSpec((1, d), lambda i, j: (0, 0)))
        args.append(g_final)
    return pl.pallas_call(
        functools.partial(_ffn_kernel, n_f=n_f, final_norm=final_norm),
        grid=(t // tm, n_f),
        in_specs=in_specs,
        out_specs=pl.BlockSpec((tm, d), lambda i, j: (i, 0)),
        out_shape=jax.ShapeDtypeStruct((t, d), F32),
        scratch_shapes=[pltpu.VMEM((tm, d), BF16)],
        compiler_params=pltpu.CompilerParams(
            dimension_semantics=("arbitrary", "arbitrary"),
            vmem_limit_bytes=48 * MIB),
        name="ffn_final" if final_norm else "ffn",
    )(*args)


def _in_proj_kernel(x_ref, g_ref, w_ref, o_ref, h_ref):
    @pl.when(pl.program_id(1) == 0)
    def _():
        h_ref[...] = _rmsnorm(x_ref[...], g_ref[...]).astype(BF16)

    o_ref[...] = jnp.dot(h_ref[...], w_ref[...], preferred_element_type=F32).astype(BF16)


def _in_proj(x, g, w, *, tm=1024, tn=1024):
    t, d = x.shape
    n = w.shape[1]
    return pl.pallas_call(
        _in_proj_kernel,
        grid=(t // tm, n // tn),
        in_specs=[
            pl.BlockSpec((tm, d), lambda i, j: (i, 0)),
            pl.BlockSpec((1, d), lambda i, j: (0, 0)),
            pl.BlockSpec((d, tn), lambda i, j: (0, j)),
        ],
        out_specs=pl.BlockSpec((tm, tn), lambda i, j: (i, j)),
        out_shape=jax.ShapeDtypeStruct((t, n), BF16),
        scratch_shapes=[pltpu.VMEM((tm, d), BF16)],
        compiler_params=pltpu.CompilerParams(
            dimension_semantics=("arbitrary", "arbitrary"),
            vmem_limit_bytes=48 * MIB),
        name="in_proj",
    )(x, g, w)


_LOG_GAMMA = tuple(math.log(1.0 - 2.0 ** (-5.0 - h)) for h in range(HEADS))


def _dot_nt(a, b):
    return lax.dot_general(a, b, (((1,), (1,)), ((), ())), preferred_element_type=F32)


def _dot_tn(a, b):
    return lax.dot_general(a, b, (((0,), (0,)), ((), ())), preferred_element_type=F32)


def _mixer_kernel(qa_ref, ka0_ref, ka1_ref, ka2_ref, va0_ref, va1_ref, va2_ref,
                  qr_ref, kr_ref, vr_ref, gr_ref, cos_ref, sin_ref, bias_ref,
                  att_ref, ret_ref,
                  decay_ref, qdec_ref, kdec_ref, state_ref):
    b = pl.program_id(0)
    i = pl.program_id(1)
    ka_refs = (ka0_ref, ka1_ref, ka2_ref)
    va_refs = (va0_ref, va1_ref, va2_ref)

    @pl.when((b == 0) & (i == 0))
    def _():
        n = lax.broadcasted_iota(jnp.int32, (TQ, TQ), 0)
        m = lax.broadcasted_iota(jnp.int32, (TQ, TQ), 1)
        dist = jnp.abs(n - m).astype(F32)
        visible = (m >> 6) <= (n >> 6)
        row = lax.broadcasted_iota(jnp.int32, (TQ, HEAD_DIM), 0).astype(F32)
        for h in range(HEADS):
            lg = _LOG_GAMMA[h]
            decay_ref[h] = jnp.where(visible, jnp.exp(lg * dist), 0.0)
            qdec_ref[h] = jnp.exp(lg * (row + 1.0))
            kdec_ref[h] = jnp.exp(lg * (TQ - 1.0 - row))

    @pl.when(i == 0)
    def _():
        state_ref[...] = jnp.zeros_like(state_ref)

    att_scale = HEAD_DIM ** -0.5
    ret_scale = HEAD_DIM ** -0.5
    cos = cos_ref[...]
    sin = sin_ref[...]

    def rot(x):
        return x * cos + pltpu.roll(x, HEAD_DIM // 2, axis=1) * sin

    for h in range(HEADS):
        cols = slice(h * HEAD_DIM, (h + 1) * HEAD_DIM)

        q = qa_ref[:, cols]
        s = []
        for j in range(N_KBLK):
            sj = _dot_nt(q, ka_refs[j][:, cols]) * att_scale + bias_ref[h, :, j * TQ:(j + 1) * TQ]
            if j < N_KBLK - 1:
                sj = jnp.where(i + j >= N_KBLK - 1, sj, NEG)
            s.append(sj)
        mx = jnp.maximum(jnp.maximum(jnp.max(s[0], axis=-1, keepdims=True),
                                     jnp.max(s[1], axis=-1, keepdims=True)),
                         jnp.max(s[2], axis=-1, keepdims=True))
        acc = None
        den = None
        for j in range(N_KBLK):
            p = jnp.exp(s[j] - mx)
            dj = jnp.sum(p, axis=-1, keepdims=True)
            oj = jnp.dot(p.astype(BF16), va_refs[j][:, cols], preferred_element_type=F32)
            acc = oj if acc is None else acc + oj
            den = dj if den is None else den + dj
        att_ref[:, cols] = (acc / den).astype(BF16)

        qr = (rot(qr_ref[:, cols].astype(F32)) * ret_scale).astype(BF16)
        kr_f = rot(kr_ref[:, cols].astype(F32))
        kr = kr_f.astype(BF16)
        v = vr_ref[:, cols]
        sr = _dot_nt(qr, kr) * decay_ref[h]
        inner = jnp.dot(sr.astype(BF16), v, preferred_element_type=F32)
        st = state_ref[h]
        cross = jnp.dot(qr, st.astype(BF16), preferred_element_type=F32) * qdec_ref[h]
        kd = (kr_f * kdec_ref[h]).astype(BF16)
        state_ref[h] = st * math.exp(_LOG_GAMMA[h] * TQ) + _dot_tn(kd, v)
        out = inner + cross
        out = out * lax.rsqrt(jnp.mean(out * out, axis=-1, keepdims=True) + EPS)
        ret_ref[:, cols] = (_silu(gr_ref[:, cols].astype(F32)) * out).astype(BF16)


def _mixer(proj, cos2, sin2, bias, batch, seq):
    n_t = seq // TQ
    proj3 = proj.reshape(batch, seq, proj.shape[-1])

    def tile(col):
        return pl.BlockSpec((None, TQ, WIDTH), lambda b, i: (b, i, col))

    def hist(col, j):
        return pl.BlockSpec((None, TQ, WIDTH),
                            lambda b, i: (b, jnp.maximum(i - (N_KBLK - 1) + j, 0), col))

    in_specs = ([tile(0)] + [hist(1, j) for j in range(N_KBLK)] + [hist(2, j) for j in range(N_KBLK)]
                + [tile(3), tile(4), tile(5), tile(6)]
                + [pl.BlockSpec((TQ, HEAD_DIM), lambda b, i: (i, 0)),
                   pl.BlockSpec((TQ, HEAD_DIM), lambda b, i: (i, 0)),
                   pl.BlockSpec((HEADS, TQ, N_KBLK * TQ), lambda b, i: (0, 0, 0))])
    out_spec = pl.BlockSpec((None, TQ, WIDTH), lambda b, i: (b, i, 0))
    att, ret = pl.pallas_call(
        _mixer_kernel,
        grid=(batch, n_t),
        in_specs=in_specs,
        out_specs=[out_spec, out_spec],
        out_shape=[jax.ShapeDtypeStruct((batch, seq, WIDTH), BF16)] * 2,
        scratch_shapes=[pltpu.VMEM((HEADS, TQ, TQ), F32),
                        pltpu.VMEM((HEADS, TQ, HEAD_DIM), F32),
                        pltpu.VMEM((HEADS, TQ, HEAD_DIM), F32),
                        pltpu.VMEM((HEADS, HEAD_DIM, HEAD_DIM), F32)],
        compiler_params=pltpu.CompilerParams(
            dimension_semantics=("arbitrary", "arbitrary"),
            vmem_limit_bytes=48 * MIB),
        name="mixer",
    )(*([proj3] * 11), cos2, sin2, bias)
    return att.reshape(batch * seq, WIDTH), ret.reshape(batch * seq, WIDTH)


def _out_proj_kernel(att_ref, ret_ref, ga0_ref, ga1_ref, gr0_ref, gr1_ref, x_ref,
                     woa_ref, wor_ref, wo_ref, o_ref, merged_ref):
    att = att_ref[...]
    ret = ret_ref[...]
    for c, (ga_ref, gr_ref) in enumerate(((ga0_ref, gr0_ref), (ga1_ref, gr1_ref))):
        cols = slice(c * WIDTH, (c + 1) * WIDTH)
        branch_a = jnp.dot(att, woa_ref[:, cols], preferred_element_type=F32)
        branch_r = jnp.dot(ret, wor_ref[:, cols], preferred_element_type=F32)
        merged = (jax.nn.sigmoid(ga_ref[...].astype(F32)) * branch_a
                  + jax.nn.sigmoid(gr_ref[...].astype(F32)) * branch_r)
        merged_ref[:, cols] = merged.astype(BF16)
    o_ref[...] = x_ref[...] + jnp.dot(merged_ref[...], wo_ref[...], preferred_element_type=F32)


def _out_proj(att, ret, proj, x, woa, wor, wo, *, tm=256):
    t, d = x.shape
    gate0 = 7

    def act(col):
        return pl.BlockSpec((tm, WIDTH), lambda i: (i, col))

    def whole(shape):
        return pl.BlockSpec(shape, lambda i: (0, 0))

    return pl.pallas_call(
        _out_proj_kernel,
        grid=(t // tm,),
        in_specs=[act(0), act(0), act(gate0), act(gate0 + 1), act(gate0 + 2), act(gate0 + 3),
                  pl.BlockSpec((tm, d), lambda i: (i, 0)),
                  whole(woa.shape), whole(wor.shape), whole(wo.shape)],
        out_specs=pl.BlockSpec((tm, d), lambda i: (i, 0)),
        out_shape=jax.ShapeDtypeStruct((t, d), F32),
        scratch_shapes=[pltpu.VMEM((tm, d), BF16)],
        compiler_params=pltpu.CompilerParams(
            dimension_semantics=("arbitrary",),
            vmem_limit_bytes=56 * MIB),
        name="out_proj",
    )(att, ret, proj, proj, proj, proj, x, woa, wor, wo)


def _rotary_tables(seq):
    half = HEAD_DIM // 2
    inv = 1.0 / (ROPE_BASE ** (jnp.arange(0, HEAD_DIM, 2, dtype=F32) / HEAD_DIM))
    ang = jnp.arange(seq, dtype=F32)[:, None] * inv[None, :]
    cos = jnp.cos(ang)
    sin = jnp.sin(ang)
    del half
    return jnp.concatenate([cos, cos], axis=-1), jnp.concatenate([-sin, sin], axis=-1)


def _band_bias(rel_bias):
    n = np.arange(TQ)[:, None]
    m = np.arange(N_KBLK * TQ)[None, :]
    rel_idx = np.clip(n - m + (N_KBLK - 1) * TQ, -MAX_REL_DIST, MAX_REL_DIST) + MAX_REL_DIST
    q_chunk = n // CHUNK
    k_chunk = m // CHUNK
    valid = (k_chunk >= q_chunk) & (k_chunk <= q_chunk + N_PREV_CHUNKS)
    return jnp.where(valid[None], rel_bias.astype(F32)[:, rel_idx], NEG)


def kernel(x, norm_ffn1_g, ffn1_w_gate, ffn1_w_up, ffn1_w_down, norm_mix_g, w_in, rel_bias,
           w_out_att, w_out_ret, w_out, norm_ffn2_g, ffn2_w_gate, ffn2_w_up, ffn2_w_down,
           norm_final_g):
    batch, seq, d = x.shape
    depth = w_in.shape[0]
    assert (N_KBLK - 1) * TQ == N_PREV_CHUNKS * CHUNK and seq % TQ == 0
    cos2, sin2 = _rotary_tables(seq)
    xt = x.reshape(batch * seq, d)
    for l in range(depth):
        last = l == depth - 1
        xt = _ffn(xt, norm_ffn1_g[l][None], ffn1_w_gate[l].astype(BF16), ffn1_w_up[l].astype(BF16),
                  ffn1_w_down[l].astype(BF16))
        proj = _in_proj(xt, norm_mix_g[l][None], w_in[l].astype(BF16))
        att, ret = _mixer(proj, cos2, sin2, _band_bias(rel_bias[l]), batch, seq)
        xt = _out_proj(att, ret, proj, xt, w_out_att[l].astype(BF16), w_out_ret[l].astype(BF16),
                       w_out[l].astype(BF16))
        xt = _ffn(xt, norm_ffn2_g[l][None], ffn2_w_gate[l].astype(BF16), ffn2_w_up[l].astype(BF16),
                  ffn2_w_down[l].astype(BF16), norm_final_g[None] if last else None)
    if depth == 0:
        xt = _rmsnorm(xt, norm_final_g[None])
    return xt.reshape(batch, seq, d)
```

```python
import functools
import math

import numpy as np
import jax
import jax.numpy as jnp
from jax import lax
from jax.experimental import pallas as pl
from jax.experimental.pallas import tpu as pltpu

F32 = jnp.float32
BF16 = jnp.bfloat16

CHUNK = 64
N_PREV_CHUNKS = 8
HEADS = 8
HEAD_DIM = 128
WIDTH = HEADS * HEAD_DIM
MAX_REL_DIST = 128
ROPE_BASE = 10000.0
EPS = 1e-6
NEG = -1e30

TQ = 256
N_KBLK = 3
MIB = 1024 * 1024


def _rmsnorm(x, g):
    ms = jnp.mean(x * x, axis=-1, keepdims=True)
    return x * lax.rsqrt(ms + EPS) * g


def _silu(x):
    return x * jax.nn.sigmoid(x)


def _ffn_kernel(*refs, n_f, final_norm):
    if final_norm:
        x_ref, g_ref, wg_ref, wu_ref, wd_ref, gf_ref, o_ref, h_ref = refs
    else:
        x_ref, g_ref, wg_ref, wu_ref, wd_ref, o_ref, h_ref = refs
    j = pl.program_id(1)

    @pl.when(j == 0)
    def _():
        h_ref[...] = _rmsnorm(x_ref[...], g_ref[...]).astype(BF16)
        o_ref[...] = jnp.zeros_like(o_ref)

    h = h_ref[...]
    gate = jnp.dot(h, wg_ref[...], preferred_element_type=F32)
    up = jnp.dot(h, wu_ref[...], preferred_element_type=F32)
    act = (_silu(gate) * up).astype(BF16)
    o_ref[...] += jnp.dot(act, wd_ref[...], preferred_element_type=F32)

    @pl.when(j == n_f - 1)
    def _():
        y = x_ref[...] + 0.5 * o_ref[...]
        if final_norm:
            y = _rmsnorm(y, gf_ref[...])
        o_ref[...] = y


def _ffn(x, g, wg, wu, wd, g_final=None, *, tm=512, tf=512):
    t, d = x.shape
    f = wg.shape[1]
    n_f = f // tf
    final_norm = g_final is not None
    in_specs = [
        pl.BlockSpec((tm, d), lambda i, j: (i, 0)),
        pl.BlockSpec((1, d), lambda i, j: (0, 0)),
        pl.BlockSpec((d, tf), lambda i, j: (0, j)),
        pl.BlockSpec((d, tf), lambda i, j: (0, j)),
        pl.BlockSpec((tf, d), lambda i, j: (j, 0)),
    ]
    args = [x, g, wg, wu, wd]
    if final_norm:
        in_specs.append(pl.BlockSpec((1, d), lambda i, j: (0, 0)))
        args.append(g_final)
    return pl.pallas_call(
        functools.partial(_ffn_kernel, n_f=n_f, final_norm=final_norm),
        grid=(t // tm, n_f),
        in_specs=in_specs,
        out_specs=pl.BlockSpec((tm, d), lambda i, j: (i, 0)),
        out_shape=jax.ShapeDtypeStruct((t, d), F32),
        scratch_shapes=[pltpu.VMEM((tm, d), BF16)],
        compiler_params=pltpu.CompilerParams(
            dimension_semantics=("arbitrary", "arbitrary"),
            vmem_limit_bytes=48 * MIB),
        name="ffn_final" if final_norm else "ffn",
    )(*args)


def _in_proj_kernel(x_ref, g_ref, w_ref, o_ref, h_ref):
    @pl.when(pl.program_id(1) == 0)
    def _():
        h_ref[...] = _rmsnorm(x_ref[...], g_ref[...]).astype(BF16)

    o_ref[...] = jnp.dot(h_ref[...], w_ref[...], preferred_element_type=F32).astype(BF16)


def _in_proj(x, g, w, *, tm=1024, tn=1024):
    t, d = x.shape
    n = w.shape[1]
    return pl.pallas_call(
        _in_proj_kernel,
        grid=(t // tm, n // tn),
        in_specs=[
            pl.BlockSpec((tm, d), lambda i, j: (i, 0)),
            pl.BlockSpec((1, d), lambda i, j: (0, 0)),
            pl.BlockSpec((d, tn), lambda i, j: (0, j)),
        ],
        out_specs=pl.BlockSpec((tm, tn), lambda i, j: (i, j)),
        out_shape=jax.ShapeDtypeStruct((t, n), BF16),
        scratch_shapes=[pltpu.VMEM((tm, d), BF16)],
        compiler_params=pltpu.CompilerParams(
            dimension_semantics=("arbitrary", "arbitrary"),
            vmem_limit_bytes=48 * MIB),
        name="in_proj",
    )(x, g, w)


_LOG_GAMMA = tuple(math.log(1.0 - 2.0 ** (-5.0 - h)) for h in range(HEADS))


def _dot_nt(a, b):
    return lax.dot_general(a, b, (((1,), (1,)), ((), ())), preferred_element_type=F32)


def _dot_tn(a, b):
    return lax.dot_general(a, b, (((0,), (0,)), ((), ())), preferred_element_type=F32)


def _mixer_kernel(qa_ref, ka0_ref, ka1_ref, ka2_ref, va0_ref, va1_ref, va2_ref,
                  qr_ref, kr_ref, vr_ref, gr_ref, cos_ref, sin_ref, relrow_ref,
                  att_ref, ret_ref,
                  bias_ref, decay_ref, qdec_ref, kdec_ref, state_ref):
    b = pl.program_id(0)
    i = pl.program_id(1)
    ka_refs = (ka0_ref, ka1_ref, ka2_ref)
    va_refs = (va0_ref, va1_ref, va2_ref)

    @pl.when((b == 0) & (i == 0))
    def _():
        qc = lax.broadcasted_iota(jnp.int32, (TQ, N_KBLK * TQ), 0) >> 6
        kc = lax.broadcasted_iota(jnp.int32, (TQ, N_KBLK * TQ), 1) >> 6
        in_band = (kc >= qc) & (kc <= qc + N_PREV_CHUNKS)
        for h in range(HEADS):
            rows = jnp.broadcast_to(relrow_ref[h:h + 1, :], (TQ, (N_KBLK + 1) * TQ))
            toeplitz = pltpu.roll(rows, 0, axis=1, stride=1, stride_axis=0)
            bias_ref[h] = jnp.where(in_band, toeplitz[:, TQ:], NEG)

        n = lax.broadcasted_iota(jnp.int32, (TQ, TQ), 0)
        m = lax.broadcasted_iota(jnp.int32, (TQ, TQ), 1)
        dist = jnp.abs(n - m).astype(F32)
        visible = (m >> 6) <= (n >> 6)
        row = lax.broadcasted_iota(jnp.int32, (TQ, HEAD_DIM), 0).astype(F32)
        for h in range(HEADS):
            lg = _LOG_GAMMA[h]
            decay_ref[h] = jnp.where(visible, jnp.exp(lg * dist), 0.0)
            qdec_ref[h] = jnp.exp(lg * (row + 1.0))
            kdec_ref[h] = jnp.exp(lg * (TQ - 1.0 - row))

    @pl.when(i == 0)
    def _():
        state_ref[...] = jnp.zeros_like(state_ref)

    att_scale = HEAD_DIM ** -0.5
    ret_scale = HEAD_DIM ** -0.5
    cos = cos_ref[...]
    sin = sin_ref[...]

    def rot(x):
        return x * cos + pltpu.roll(x, HEAD_DIM // 2, axis=1) * sin

    for h in range(HEADS):
        cols = slice(h * HEAD_DIM, (h + 1) * HEAD_DIM)

        q = qa_ref[:, cols]
        s = []
        for j in range(N_KBLK):
            sj = _dot_nt(q, ka_refs[j][:, cols]) * att_scale + bias_ref[h, :, j * TQ:(j + 1) * TQ]
            if j < N_KBLK - 1:
                sj = jnp.where(i + j >= N_KBLK - 1, sj, NEG)
            s.append(sj)
        mx = jnp.maximum(jnp.maximum(jnp.max(s[0], axis=-1, keepdims=True),
                                     jnp.max(s[1], axis=-1, keepdims=True)),
                         jnp.max(s[2], axis=-1, keepdims=True))
        acc = None
        den = None
        for j in range(N_KBLK):
            p = jnp.exp(s[j] - mx)
            dj = jnp.sum(p, axis=-1, keepdims=True)
            oj = jnp.dot(p.astype(BF16), va_refs[j][:, cols], preferred_element_type=F32)
            acc = oj if acc is None else acc + oj
            den = dj if den is None else den + dj
        att_ref[:, cols] = (acc / den).astype(BF16)

        qr = (rot(qr_ref[:, cols].astype(F32)) * ret_scale).astype(BF16)
        kr_f = rot(kr_ref[:, cols].astype(F32))
        kr = kr_f.astype(BF16)
        v = vr_ref[:, cols]
        sr = _dot_nt(qr, kr) * decay_ref[h]
        inner = jnp.dot(sr.astype(BF16), v, preferred_element_type=F32)
        st = state_ref[h]
        cross = jnp.dot(qr, st.astype(BF16), preferred_element_type=F32) * qdec_ref[h]
        kd = (kr_f * kdec_ref[h]).astype(BF16)
        state_ref[h] = st * math.exp(_LOG_GAMMA[h] * TQ) + _dot_tn(kd, v)
        out = inner + cross
        out = out * lax.rsqrt(jnp.mean(out * out, axis=-1, keepdims=True) + EPS)
        ret_ref[:, cols] = (_silu(gr_ref[:, cols].astype(F32)) * out).astype(BF16)


def _mixer(proj, cos2, sin2, rel_row, batch, seq):
    n_t = seq // TQ
    proj3 = proj.reshape(batch, seq, proj.shape[-1])

    def tile(col):
        return pl.BlockSpec((None, TQ, WIDTH), lambda b, i: (b, i, col))

    def hist(col, j):
        return pl.BlockSpec((None, TQ, WIDTH),
                            lambda b, i: (b, jnp.maximum(i - (N_KBLK - 1) + j, 0), col))

    in_specs = ([tile(0)] + [hist(1, j) for j in range(N_KBLK)] + [hist(2, j) for j in range(N_KBLK)]
                + [tile(3), tile(4), tile(5), tile(6)]
                + [pl.BlockSpec((TQ, HEAD_DIM), lambda b, i: (i, 0)),
                   pl.BlockSpec((TQ, HEAD_DIM), lambda b, i: (i, 0)),
                   pl.BlockSpec((HEADS, (N_KBLK + 1) * TQ), lambda b, i: (0, 0))])
    out_spec = pl.BlockSpec((None, TQ, WIDTH), lambda b, i: (b, i, 0))
    att, ret = pl.pallas_call(
        _mixer_kernel,
        grid=(batch, n_t),
        in_specs=in_specs,
        out_specs=[out_spec, out_spec],
        out_shape=[jax.ShapeDtypeStruct((batch, seq, WIDTH), BF16)] * 2,
        scratch_shapes=[pltpu.VMEM((HEADS, TQ, N_KBLK * TQ), F32),
                        pltpu.VMEM((HEADS, TQ, TQ), F32),
                        pltpu.VMEM((HEADS, TQ, HEAD_DIM), F32),
                        pltpu.VMEM((HEADS, TQ, HEAD_DIM), F32),
                        pltpu.VMEM((HEADS, HEAD_DIM, HEAD_DIM), F32)],
        compiler_params=pltpu.CompilerParams(
            dimension_semantics=("arbitrary", "arbitrary"),
            vmem_limit_bytes=48 * MIB),
        name="mixer",
    )(*([proj3] * 11), cos2, sin2, rel_row)
    return att.reshape(batch * seq, WIDTH), ret.reshape(batch * seq, WIDTH)


def _out_proj_kernel(att_ref, ret_ref, ga0_ref, ga1_ref, gr0_ref, gr1_ref, x_ref,
                     woa_ref, wor_ref, wo_ref, o_ref, merged_ref):
    att = att_ref[...]
    ret = ret_ref[...]
    for c, (ga_ref, gr_ref) in enumerate(((ga0_ref, gr0_ref), (ga1_ref, gr1_ref))):
        cols = slice(c * WIDTH, (c + 1) * WIDTH)
        branch_a = jnp.dot(att, woa_ref[:, cols], preferred_element_type=F32)
        branch_r = jnp.dot(ret, wor_ref[:, cols], preferred_element_type=F32)
        merged = (jax.nn.sigmoid(ga_ref[...].astype(F32)) * branch_a
                  + jax.nn.sigmoid(gr_ref[...].astype(F32)) * branch_r)
        merged_ref[:, cols] = merged.astype(BF16)
    o_ref[...] = x_ref[...] + jnp.dot(merged_ref[...], wo_ref[...], preferred_element_type=F32)


def _out_proj(att, ret, proj, x, woa, wor, wo, *, tm=256):
    t, d = x.shape
    gate0 = 7

    def act(col):
        return pl.BlockSpec((tm, WIDTH), lambda i: (i, col))

    def whole(shape):
        return pl.BlockSpec(shape, lambda i: (0, 0))

    return pl.pallas_call(
        _out_proj_kernel,
        grid=(t // tm,),
        in_specs=[act(0), act(0), act(gate0), act(gate0 + 1), act(gate0 + 2), act(gate0 + 3),
                  pl.BlockSpec((tm, d), lambda i: (i, 0)),
                  whole(woa.shape), whole(wor.shape), whole(wo.shape)],
        out_specs=pl.BlockSpec((tm, d), lambda i: (i, 0)),
        out_shape=jax.ShapeDtypeStruct((t, d), F32),
        scratch_shapes=[pltpu.VMEM((tm, d), BF16)],
        compiler_params=pltpu.CompilerParams(
            dimension_semantics=("arbitrary",),
            vmem_limit_bytes=56 * MIB),
        name="out_proj",
    )(att, ret, proj, proj, proj, proj, x, woa, wor, wo)


def _rotary_tables(seq):
    inv = 1.0 / (ROPE_BASE ** (jnp.arange(0, HEAD_DIM, 2, dtype=F32) / HEAD_DIM))
    ang = jnp.arange(seq, dtype=F32)[:, None] * inv[None, :]
    cos = jnp.cos(ang)
    sin = jnp.sin(ang)
    return jnp.concatenate([cos, cos], axis=-1), jnp.concatenate([-sin, sin], axis=-1)


def _rel_row(rel_bias):
    j = np.arange((N_KBLK + 1) * TQ)
    rel_idx = np.clip(N_KBLK * TQ - j, -MAX_REL_DIST, MAX_REL_DIST) + MAX_REL_DIST
    return rel_bias.astype(F32)[:, rel_idx]


def kernel(x, norm_ffn1_g, ffn1_w_gate, ffn1_w_up, ffn1_w_down, norm_mix_g, w_in, rel_bias,
           w_out_att, w_out_ret, w_out, norm_ffn2_g, ffn2_w_gate, ffn2_w_up, ffn2_w_down,
           norm_final_g):
    batch, seq, d = x.shape
    depth = w_in.shape[0]
    assert depth >= 1 and (N_KBLK - 1) * TQ == N_PREV_CHUNKS * CHUNK and seq % TQ == 0
    cos2, sin2 = _rotary_tables(seq)
    xt = x.reshape(batch * seq, d)
    for l in range(depth):
        last = l == depth - 1
        xt = _ffn(xt, norm_ffn1_g[l][None], ffn1_w_gate[l].astype(BF16), ffn1_w_up[l].astype(BF16),
                  ffn1_w_down[l].astype(BF16))
        proj = _in_proj(xt, norm_mix_g[l][None], w_in[l].astype(BF16))
        att, ret = _mixer(proj, cos2, sin2, _rel_row(rel_bias[l]), batch, seq)
        xt = _out_proj(att, ret, proj, xt, w_out_att[l].astype(BF16), w_out_ret[l].astype(BF16),
                       w_out[l].astype(BF16))
        xt = _ffn(xt, norm_ffn2_g[l][None], ffn2_w_gate[l].astype(BF16), ffn2_w_up[l].astype(BF16),
                  ffn2_w_down[l].astype(BF16), norm_final_g[None] if last else None)
    return xt.reshape(batch, seq, d)
```

```python
import functools
import math

import numpy as np
import jax
import jax.numpy as jnp
from jax import lax
from jax.experimental import pallas as pl
from jax.experimental.pallas import tpu as pltpu

F32 = jnp.float32
BF16 = jnp.bfloat16

CHUNK = 64
N_PREV_CHUNKS = 8
HEADS = 8
HEAD_DIM = 128
WIDTH = HEADS * HEAD_DIM
MAX_REL_DIST = 128
ROPE_BASE = 10000.0
EPS = 1e-6
NEG = -1e30
LOG2E = math.log2(math.e)

TQ = 256
N_KBLK = 3
MIB = 1024 * 1024


def _rmsnorm(x, g):
    ms = jnp.mean(x * x, axis=-1, keepdims=True)
    return x * lax.rsqrt(ms + EPS) * g


def _silu(x):
    return x * jax.nn.sigmoid(x)


def _ffn_kernel(*refs, n_f, final_norm):
    if final_norm:
        x_ref, g_ref, wg_ref, wu_ref, wd_ref, gf_ref, o_ref, h_ref = refs
    else:
        x_ref, g_ref, wg_ref, wu_ref, wd_ref, o_ref, h_ref = refs
    j = pl.program_id(1)

    def partial_down(h):
        gate = jnp.dot(h, wg_ref[...], preferred_element_type=F32)
        up = jnp.dot(h, wu_ref[...], preferred_element_type=F32)
        act = (_silu(gate) * up).astype(BF16)
        return jnp.dot(act, wd_ref[...], preferred_element_type=F32)

    @pl.when(j == 0)
    def _():
        h = _rmsnorm(x_ref[...], g_ref[...]).astype(BF16)
        h_ref[...] = h
        o_ref[...] = partial_down(h)

    @pl.when(j > 0)
    def _():
        o_ref[...] += partial_down(h_ref[...])

    @pl.when(j == n_f - 1)
    def _():
        y = x_ref[...] + 0.5 * o_ref[...]
        if final_norm:
            y = _rmsnorm(y, gf_ref[...])
        o_ref[...] = y


def _ffn(x, g, wg, wu, wd, g_final=None, *, tm=1024, tf=512):
    t, d = x.shape
    f = wg.shape[1]
    n_f = f // tf
    final_norm = g_final is not None
    in_specs = [
        pl.BlockSpec((tm, d), lambda i, j: (i, 0)),
        pl.BlockSpec((1, d), lambda i, j: (0, 0)),
        pl.BlockSpec((d, tf), lambda i, j: (0, j)),
        pl.BlockSpec((d, tf), lambda i, j: (0, j)),
        pl.BlockSpec((tf, d), lambda i, j: (j, 0)),
    ]
    args = [x, g, wg, wu, wd]
    if final_norm:
        in_specs.append(pl.BlockSpec((1, d), lambda i, j: (0, 0)))
        args.append(g_final)
    return pl.pallas_call(
        functools.partial(_ffn_kernel, n_f=n_f, final_norm=final_norm),
        grid=(t // tm, n_f),
        in_specs=in_specs,
        out_specs=pl.BlockSpec((tm, d), lambda i, j: (i, 0)),
        out_shape=jax.ShapeDtypeStruct((t, d), F32),
        scratch_shapes=[pltpu.VMEM((tm, d), BF16)],
        compiler_params=pltpu.CompilerParams(
            dimension_semantics=("arbitrary", "arbitrary"),
            vmem_limit_bytes=60 * MIB),
        name="ffn_final" if final_norm else "ffn",
    )(*args)


def _in_proj_kernel(x_ref, g_ref, w_ref, o_ref, h_ref, *, first_tile_scale):
    j = pl.program_id(1)

    def project(h):
        return jnp.dot(h, w_ref[...], preferred_element_type=F32)

    @pl.when(j == 0)
    def _():
        h = _rmsnorm(x_ref[...], g_ref[...]).astype(BF16)
        h_ref[...] = h
        o_ref[...] = (project(h) * first_tile_scale).astype(BF16)

    @pl.when(j > 0)
    def _():
        o_ref[...] = project(h_ref[...]).astype(BF16)


def _in_proj(x, g, w, *, first_tile_scale, tm=1024, tn=WIDTH):
    t, d = x.shape
    n = w.shape[1]
    return pl.pallas_call(
        functools.partial(_in_proj_kernel, first_tile_scale=first_tile_scale),
        grid=(t // tm, n // tn),
        in_specs=[
            pl.BlockSpec((tm, d), lambda i, j: (i, 0)),
            pl.BlockSpec((1, d), lambda i, j: (0, 0)),
            pl.BlockSpec((d, tn), lambda i, j: (0, j)),
        ],
        out_specs=pl.BlockSpec((tm, tn), lambda i, j: (i, j)),
        out_shape=jax.ShapeDtypeStruct((t, n), BF16),
        scratch_shapes=[pltpu.VMEM((tm, d), BF16)],
        compiler_params=pltpu.CompilerParams(
            dimension_semantics=("arbitrary", "arbitrary"),
            vmem_limit_bytes=48 * MIB),
        name="in_proj",
    )(x, g, w)


_LOG_GAMMA = tuple(math.log(1.0 - 2.0 ** (-5.0 - h)) for h in range(HEADS))


def _dot_nt(a, b):
    return lax.dot_general(a, b, (((1,), (1,)), ((), ())), preferred_element_type=F32)


def _dot_tn(a, b):
    return lax.dot_general(a, b, (((0,), (0,)), ((), ())), preferred_element_type=F32)


def _mixer_kernel(qa_ref, ka0_ref, ka1_ref, ka2_ref, va0_ref, va1_ref, va2_ref,
                  qr_ref, kr_ref, vr_ref, gr_ref, cos_ref, sin_ref, relrow_ref,
                  att_ref, ret_ref,
                  bias_ref, decay_ref, qdec_ref, kdec_ref, state_ref):
    b = pl.program_id(0)
    i = pl.program_id(1)
    ka_refs = (ka0_ref, ka1_ref, ka2_ref)
    va_refs = (va0_ref, va1_ref, va2_ref)

    @pl.when((b == 0) & (i == 0))
    def _():
        qc = lax.broadcasted_iota(jnp.int32, (TQ, N_KBLK * TQ), 0) >> 6
        kc = lax.broadcasted_iota(jnp.int32, (TQ, N_KBLK * TQ), 1) >> 6
        in_band = (kc >= qc) & (kc <= qc + N_PREV_CHUNKS)
        for h in range(HEADS):
            rows = jnp.broadcast_to(relrow_ref[h:h + 1, :], (TQ, (N_KBLK + 1) * TQ))
            toeplitz = pltpu.roll(rows, 0, axis=1, stride=1, stride_axis=0)
            bias_ref[h] = jnp.where(in_band, toeplitz[:, TQ:] * LOG2E, NEG)
        bias_ref[HEADS] = jnp.full((TQ, N_KBLK * TQ), NEG, F32)

        n = lax.broadcasted_iota(jnp.int32, (TQ, TQ), 0)
        m = lax.broadcasted_iota(jnp.int32, (TQ, TQ), 1)
        dist = jnp.abs(n - m).astype(F32)
        visible = (m >> 6) <= (n >> 6)
        row = lax.broadcasted_iota(jnp.int32, (TQ, HEAD_DIM), 0).astype(F32)
        for h in range(HEADS):
            lg = _LOG_GAMMA[h]
            decay_ref[h] = jnp.where(visible, jnp.exp(lg * dist), 0.0)
            qdec_ref[h] = jnp.exp(lg * (row + 1.0))
            kdec_ref[h] = jnp.exp(lg * (TQ - 1.0 - row))

    @pl.when(i == 0)
    def _():
        state_ref[...] = jnp.zeros_like(state_ref)

    ret_scale = HEAD_DIM ** -0.5
    cos = cos_ref[...]
    sin = sin_ref[...]

    def rot(x):
        return x * cos + pltpu.roll(x, HEAD_DIM // 2, axis=1) * sin

    for h in range(HEADS):
        cols = slice(h * HEAD_DIM, (h + 1) * HEAD_DIM)

        q = qa_ref[:, cols]
        s = []
        for j in range(N_KBLK):
            plane = h if j == N_KBLK - 1 else jnp.where(i + j >= N_KBLK - 1, h, HEADS)
            s.append(_dot_nt(q, ka_refs[j][:, cols]) + bias_ref[plane, :, j * TQ:(j + 1) * TQ])
        mx = jnp.maximum(jnp.maximum(jnp.max(s[0], axis=-1, keepdims=True),
                                     jnp.max(s[1], axis=-1, keepdims=True)),
                         jnp.max(s[2], axis=-1, keepdims=True))
        acc = None
        den = None
        for j in range(N_KBLK):
            p = jnp.exp2(s[j] - mx)
            dj = jnp.sum(p, axis=-1, keepdims=True)
            oj = jnp.dot(p.astype(BF16), va_refs[j][:, cols], preferred_element_type=F32)
            acc = oj if acc is None else acc + oj
            den = dj if den is None else den + dj
        att_ref[:, cols] = (acc / den).astype(BF16)

        qr = (rot(qr_ref[:, cols].astype(F32)) * ret_scale).astype(BF16)
        kr_f = rot(kr_ref[:, cols].astype(F32))
        kr = kr_f.astype(BF16)
        v = vr_ref[:, cols]
        sr = _dot_nt(qr, kr) * decay_ref[h]
        inner = jnp.dot(sr.astype(BF16), v, preferred_element_type=F32)
        st = state_ref[h]
        cross = jnp.dot(qr, st.astype(BF16), preferred_element_type=F32) * qdec_ref[h]
        kd = (kr_f * kdec_ref[h]).astype(BF16)
        state_ref[h] = st * math.exp(_LOG_GAMMA[h] * TQ) + _dot_tn(kd, v)
        out = inner + cross
        out = out * lax.rsqrt(jnp.mean(out * out, axis=-1, keepdims=True) + EPS)
        ret_ref[:, cols] = (_silu(gr_ref[:, cols].astype(F32)) * out).astype(BF16)


def _mixer(proj, cos2, sin2, rel_row, batch, seq):
    n_t = seq // TQ
    proj3 = proj.reshape(batch, seq, proj.shape[-1])

    def tile(col):
        return pl.BlockSpec((None, TQ, WIDTH), lambda b, i: (b, i, col))

    def hist(col, j):
        return pl.BlockSpec((None, TQ, WIDTH),
                            lambda b, i: (b, jnp.maximum(i - (N_KBLK - 1) + j, 0), col))

    in_specs = ([tile(0)] + [hist(1, j) for j in range(N_KBLK)] + [hist(2, j) for j in range(N_KBLK)]
                + [tile(3), tile(4), tile(5), tile(6)]
                + [pl.BlockSpec((TQ, HEAD_DIM), lambda b, i: (i, 0)),
                   pl.BlockSpec((TQ, HEAD_DIM), lambda b, i: (i, 0)),
                   pl.BlockSpec((HEADS, (N_KBLK + 1) * TQ), lambda b, i: (0, 0))])
    out_spec = pl.BlockSpec((None, TQ, WIDTH), lambda b, i: (b, i, 0))
    att, ret = pl.pallas_call(
        _mixer_kernel,
        grid=(batch, n_t),
        in_specs=in_specs,
        out_specs=[out_spec, out_spec],
        out_shape=[jax.ShapeDtypeStruct((batch, seq, WIDTH), BF16)] * 2,
        scratch_shapes=[pltpu.VMEM((HEADS + 1, TQ, N_KBLK * TQ), F32),
                        pltpu.VMEM((HEADS, TQ, TQ), F32),
                        pltpu.VMEM((HEADS, TQ, HEAD_DIM), F32),
                        pltpu.VMEM((HEADS, TQ, HEAD_DIM), F32),
                        pltpu.VMEM((HEADS, HEAD_DIM, HEAD_DIM), F32)],
        compiler_params=pltpu.CompilerParams(
            dimension_semantics=("arbitrary", "arbitrary"),
            vmem_limit_bytes=48 * MIB),
        name="mixer",
    )(*([proj3] * 11), cos2, sin2, rel_row)
    return att.reshape(batch * seq, WIDTH), ret.reshape(batch * seq, WIDTH)


def _out_proj_kernel(att_ref, ret_ref, ga0_ref, ga1_ref, gr0_ref, gr1_ref, x_ref,
                     woa_ref, wor_ref, wo_ref, o_ref, merged_ref):
    att = att_ref[...]
    ret = ret_ref[...]
    for c, (ga_ref, gr_ref) in enumerate(((ga0_ref, gr0_ref), (ga1_ref, gr1_ref))):
        cols = slice(c * WIDTH, (c + 1) * WIDTH)
        branch_a = jnp.dot(att, woa_ref[:, cols], preferred_element_type=F32)
        branch_r = jnp.dot(ret, wor_ref[:, cols], preferred_element_type=F32)
        merged = (jax.nn.sigmoid(ga_ref[...].astype(F32)) * branch_a
                  + jax.nn.sigmoid(gr_ref[...].astype(F32)) * branch_r)
        merged_ref[:, cols] = merged.astype(BF16)
    o_ref[...] = x_ref[...] + jnp.dot(merged_ref[...], wo_ref[...], preferred_element_type=F32)


def _out_proj(att, ret, proj, x, woa, wor, wo, *, tm=256):
    t, d = x.shape
    gate0 = 7

    def act(col):
        return pl.BlockSpec((tm, WIDTH), lambda i: (i, col))

    def whole(shape):
        return pl.BlockSpec(shape, lambda i: (0, 0))

    return pl.pallas_call(
        _out_proj_kernel,
        grid=(t // tm,),
        in_specs=[act(0), act(0), act(gate0), act(gate0 + 1), act(gate0 + 2), act(gate0 + 3),
                  pl.BlockSpec((tm, d), lambda i: (i, 0)),
                  whole(woa.shape), whole(wor.shape), whole(wo.shape)],
        out_specs=pl.BlockSpec((tm, d), lambda i: (i, 0)),
        out_shape=jax.ShapeDtypeStruct((t, d), F32),
        scratch_shapes=[pltpu.VMEM((tm, d), BF16)],
        compiler_params=pltpu.CompilerParams(
            dimension_semantics=("arbitrary",),
            vmem_limit_bytes=56 * MIB),
        name="out_proj",
    )(att, ret, proj, proj, proj, proj, x, woa, wor, wo)


def _rotary_tables(seq):
    inv = 1.0 / (ROPE_BASE ** (jnp.arange(0, HEAD_DIM, 2, dtype=F32) / HEAD_DIM))
    ang = jnp.arange(seq, dtype=F32)[:, None] * inv[None, :]
    cos = jnp.cos(ang)
    sin = jnp.sin(ang)
    return jnp.concatenate([cos, cos], axis=-1), jnp.concatenate([-sin, sin], axis=-1)


def _rel_row(rel_bias):
    j = np.arange((N_KBLK + 1) * TQ)
    rel_idx = np.clip(N_KBLK * TQ - j, -MAX_REL_DIST, MAX_REL_DIST) + MAX_REL_DIST
    return rel_bias.astype(F32)[:, rel_idx]


def kernel(x, norm_ffn1_g, ffn1_w_gate, ffn1_w_up, ffn1_w_down, norm_mix_g, w_in, rel_bias,
           w_out_att, w_out_ret, w_out, norm_ffn2_g, ffn2_w_gate, ffn2_w_up, ffn2_w_down,
           norm_final_g):
    batch, seq, d = x.shape
    depth = w_in.shape[0]
    assert depth >= 1 and (N_KBLK - 1) * TQ == N_PREV_CHUNKS * CHUNK and seq % TQ == 0
    cos2, sin2 = _rotary_tables(seq)
    xt = x.reshape(batch * seq, d)
    for l in range(depth):
        last = l == depth - 1
        xt = _ffn(xt, norm_ffn1_g[l][None], ffn1_w_gate[l].astype(BF16), ffn1_w_up[l].astype(BF16),
                  ffn1_w_down[l].astype(BF16))
        proj = _in_proj(xt, norm_mix_g[l][None], w_in[l].astype(BF16),
                        first_tile_scale=HEAD_DIM ** -0.5 * LOG2E)
        att, ret = _mixer(proj, cos2, sin2, _rel_row(rel_bias[l]), batch, seq)
        xt = _out_proj(att, ret, proj, xt, w_out_att[l].astype(BF16), w_out_ret[l].astype(BF16),
                       w_out[l].astype(BF16))
        xt = _ffn(xt, norm_ffn2_g[l][None], ffn2_w_gate[l].astype(BF16), ffn2_w_up[l].astype(BF16),
                  ffn2_w_down[l].astype(BF16), norm_final_g[None] if last else None)
    return xt.reshape(batch, seq, d)
```

```python
import functools
import math

import numpy as np
import jax
import jax.numpy as jnp
from jax import lax
from jax.experimental import pallas as pl
from jax.experimental.pallas import tpu as pltpu

F32 = jnp.float32
BF16 = jnp.bfloat16

CHUNK = 64
N_PREV_CHUNKS = 8
HEADS = 8
HEAD_DIM = 128
WIDTH = HEADS * HEAD_DIM
MAX_REL_DIST = 128
ROPE_BASE = 10000.0
EPS = 1e-6
NEG = -1e30
LOG2E = math.log2(math.e)

LANES = 128
BF16_SUBLANES = 16

TQ = 256
N_KBLK = 3
MIB = 1024 * 1024


def _rmsnorm(x, g):
    ms = jnp.mean(x * x, axis=-1, keepdims=True)
    return x * lax.rsqrt(ms + EPS) * g


def _silu(x):
    return x * jax.nn.sigmoid(x)


def _cast_plan(shape, n_i, n_j):
    r, c = shape
    if r % (n_i * BF16_SUBLANES) == 0 and c % (n_j * LANES) == 0:
        return pl.BlockSpec((r // n_i, c // n_j), lambda i, j: (i, j)), True
    if r % (n_j * BF16_SUBLANES) == 0 and c % (n_i * LANES) == 0:
        return pl.BlockSpec((r // n_j, c // n_i), lambda i, j: (j, i)), True
    assert r % (n_i * BF16_SUBLANES) == 0, shape
    return pl.BlockSpec((r // n_i, c), lambda i, j: (i, 0)), False


def _cast_side_args(weights, n_i, n_j):
    plans = [_cast_plan(w.shape, n_i, n_j) for w in weights]
    order = ([k for k, (_, every) in enumerate(plans) if every]
             + [k for k, (_, every) in enumerate(plans) if not every])
    specs = [plans[k][0] for k in order]
    n_every = sum(1 for _, every in plans if every)
    return order, specs, n_every


def _unpermute(values, order):
    out = [None] * len(order)
    for pos, k in enumerate(order):
        out[k] = values[pos]
    return out


def _cast_blocks(pairs):
    for src_ref, dst_ref in pairs:
        dst_ref[...] = src_ref[...].astype(BF16)


def _ffn_kernel(*refs, n_f, final_norm, n_cast, n_cast_every):
    n_in = 6 if final_norm else 5
    x_ref, g_ref, wg_ref, wu_ref, wd_ref = refs[:5]
    gf_ref = refs[5] if final_norm else None
    o_ref = refs[n_in + n_cast]
    h_ref = refs[-1]
    casts = list(zip(refs[n_in:n_in + n_cast], refs[n_in + n_cast + 1:n_in + 2 * n_cast + 1]))
    cast_every, cast_first = casts[:n_cast_every], casts[n_cast_every:]
    j = pl.program_id(1)

    def partial_down(h):
        gate = jnp.dot(h, wg_ref[...], preferred_element_type=F32)
        up = jnp.dot(h, wu_ref[...], preferred_element_type=F32)
        act = (_silu(gate) * up).astype(BF16)
        return jnp.dot(act, wd_ref[...], preferred_element_type=F32)

    @pl.when(j == 0)
    def _():
        _cast_blocks(cast_every + cast_first)
        h = _rmsnorm(x_ref[...], g_ref[...]).astype(BF16)
        h_ref[...] = h
        o_ref[...] = partial_down(h)

    @pl.when(j > 0)
    def _():
        _cast_blocks(cast_every)
        o_ref[...] += partial_down(h_ref[...])

    @pl.when(j == n_f - 1)
    def _():
        y = x_ref[...] + 0.5 * o_ref[...]
        if final_norm:
            y = _rmsnorm(y, gf_ref[...])
        o_ref[...] = y


def _ffn(x, g, wg, wu, wd, g_final=None, *, cast=(), tm=1024, tf=512):
    t, d = x.shape
    f = wg.shape[1]
    n_i, n_f = t // tm, f // tf
    final_norm = g_final is not None
    in_specs = [
        pl.BlockSpec((tm, d), lambda i, j: (i, 0)),
        pl.BlockSpec((1, d), lambda i, j: (0, 0)),
        pl.BlockSpec((d, tf), lambda i, j: (0, j)),
        pl.BlockSpec((d, tf), lambda i, j: (0, j)),
        pl.BlockSpec((tf, d), lambda i, j: (j, 0)),
    ]
    args = [x, g, wg, wu, wd]
    if final_norm:
        in_specs.append(pl.BlockSpec((1, d), lambda i, j: (0, 0)))
        args.append(g_final)
    order, cast_specs, n_every = _cast_side_args(cast, n_i, n_f)
    outs = pl.pallas_call(
        functools.partial(_ffn_kernel, n_f=n_f, final_norm=final_norm, n_cast=len(cast),
                          n_cast_every=n_every),
        grid=(n_i, n_f),
        in_specs=in_specs + cast_specs,
        out_specs=[pl.BlockSpec((tm, d), lambda i, j: (i, 0))] + cast_specs,
        out_shape=[jax.ShapeDtypeStruct((t, d), F32)]
        + [jax.ShapeDtypeStruct(cast[k].shape, BF16) for k in order],
        scratch_shapes=[pltpu.VMEM((tm, d), BF16)],
        compiler_params=pltpu.CompilerParams(
            dimension_semantics=("arbitrary", "arbitrary"),
            vmem_limit_bytes=60 * MIB),
        name="ffn_final" if final_norm else "ffn",
    )(*args, *[cast[k] for k in order])
    return outs[0], _unpermute(outs[1:], order)


def _in_proj_kernel(*refs, first_tile_scale, n_cast, n_cast_every):
    x_ref, g_ref, w_ref = refs[:3]
    o_ref = refs[3 + n_cast]
    h_ref = refs[-1]
    casts = list(zip(refs[3:3 + n_cast], refs[4 + n_cast:4 + 2 * n_cast]))
    cast_every, cast_first = casts[:n_cast_every], casts[n_cast_every:]
    j = pl.program_id(1)

    def project(h):
        return jnp.dot(h, w_ref[...], preferred_element_type=F32)

    @pl.when(j == 0)
    def _():
        _cast_blocks(cast_every + cast_first)
        h = _rmsnorm(x_ref[...], g_ref[...]).astype(BF16)
        h_ref[...] = h
        o_ref[...] = (project(h) * first_tile_scale).astype(BF16)

    @pl.when(j > 0)
    def _():
        _cast_blocks(cast_every)
        o_ref[...] = project(h_ref[...]).astype(BF16)


def _in_proj(x, g, w, *, first_tile_scale, cast=(), tm=1024, tn=WIDTH):
    t, d = x.shape
    n = w.shape[1]
    n_i, n_j = t // tm, n // tn
    order, cast_specs, n_every = _cast_side_args(cast, n_i, n_j)
    outs = pl.pallas_call(
        functools.partial(_in_proj_kernel, first_tile_scale=first_tile_scale, n_cast=len(cast),
                          n_cast_every=n_every),
        grid=(n_i, n_j),
        in_specs=[
            pl.BlockSpec((tm, d), lambda i, j: (i, 0)),
            pl.BlockSpec((1, d), lambda i, j: (0, 0)),
            pl.BlockSpec((d, tn), lambda i, j: (0, j)),
        ] + cast_specs,
        out_specs=[pl.BlockSpec((tm, tn), lambda i, j: (i, j))] + cast_specs,
        out_shape=[jax.ShapeDtypeStruct((t, n), BF16)]
        + [jax.ShapeDtypeStruct(cast[k].shape, BF16) for k in order],
        scratch_shapes=[pltpu.VMEM((tm, d), BF16)],
        compiler_params=pltpu.CompilerParams(
            dimension_semantics=("arbitrary", "arbitrary"),
            vmem_limit_bytes=48 * MIB),
        name="in_proj",
    )(x, g, w, *[cast[k] for k in order])
    return outs[0], _unpermute(outs[1:], order)


_LOG_GAMMA = tuple(math.log(1.0 - 2.0 ** (-5.0 - h)) for h in range(HEADS))


def _dot_nt(a, b):
    return lax.dot_general(a, b, (((1,), (1,)), ((), ())), preferred_element_type=F32)


def _dot_tn(a, b):
    return lax.dot_general(a, b, (((0,), (0,)), ((), ())), preferred_element_type=F32)


def _mixer_kernel(qa_ref, ka0_ref, ka1_ref, ka2_ref, va0_ref, va1_ref, va2_ref,
                  qr_ref, kr_ref, vr_ref, gr_ref, cos_ref, sin_ref, relrow_ref,
                  att_ref, ret_ref,
                  bias_ref, decay_ref, qdec_ref, kdec_ref, state_ref):
    b = pl.program_id(0)
    i = pl.program_id(1)
    ka_refs = (ka0_ref, ka1_ref, ka2_ref)
    va_refs = (va0_ref, va1_ref, va2_ref)

    @pl.when((b == 0) & (i == 0))
    def _():
        qc = lax.broadcasted_iota(jnp.int32, (TQ, N_KBLK * TQ), 0) >> 6
        kc = lax.broadcasted_iota(jnp.int32, (TQ, N_KBLK * TQ), 1) >> 6
        in_band = (kc >= qc) & (kc <= qc + N_PREV_CHUNKS)
        for h in range(HEADS):
            rows = jnp.broadcast_to(relrow_ref[h:h + 1, :], (TQ, (N_KBLK + 1) * TQ))
            toeplitz = pltpu.roll(rows, 0, axis=1, stride=1, stride_axis=0)
            bias_ref[h] = jnp.where(in_band, toeplitz[:, TQ:] * LOG2E, NEG)
        bias_ref[HEADS] = jnp.full((TQ, N_KBLK * TQ), NEG, F32)

        n = lax.broadcasted_iota(jnp.int32, (TQ, TQ), 0)
        m = lax.broadcasted_iota(jnp.int32, (TQ, TQ), 1)
        dist = jnp.abs(n - m).astype(F32)
        visible = (m >> 6) <= (n >> 6)
        row = lax.broadcasted_iota(jnp.int32, (TQ, HEAD_DIM), 0).astype(F32)
        for h in range(HEADS):
            lg = _LOG_GAMMA[h]
            decay_ref[h] = jnp.where(visible, jnp.exp(lg * dist), 0.0)
            qdec_ref[h] = jnp.exp(lg * (row + 1.0))
            kdec_ref[h] = jnp.exp(lg * (TQ - 1.0 - row))

    @pl.when(i == 0)
    def _():
        state_ref[...] = jnp.zeros_like(state_ref)

    ret_scale = HEAD_DIM ** -0.5
    cos = cos_ref[...]
    sin = sin_ref[...]

    def rot(x):
        return x * cos + pltpu.roll(x, HEAD_DIM // 2, axis=1) * sin

    for h in range(HEADS):
        cols = slice(h * HEAD_DIM, (h + 1) * HEAD_DIM)

        q = qa_ref[:, cols]
        s = []
        for j in range(N_KBLK):
            plane = h if j == N_KBLK - 1 else jnp.where(i + j >= N_KBLK - 1, h, HEADS)
            s.append(_dot_nt(q, ka_refs[j][:, cols]) + bias_ref[plane, :, j * TQ:(j + 1) * TQ])
        mx = jnp.maximum(jnp.maximum(jnp.max(s[0], axis=-1, keepdims=True),
                                     jnp.max(s[1], axis=-1, keepdims=True)),
                         jnp.max(s[2], axis=-1, keepdims=True))
        acc = None
        den = None
        for j in range(N_KBLK):
            p = jnp.exp2(s[j] - mx)
            dj = jnp.sum(p, axis=-1, keepdims=True)
            oj = jnp.dot(p.astype(BF16), va_refs[j][:, cols], preferred_element_type=F32)
            acc = oj if acc is None else acc + oj
            den = dj if den is None else den + dj
        att_ref[:, cols] = (acc / den).astype(BF16)

        qr = (rot(qr_ref[:, cols].astype(F32)) * ret_scale).astype(BF16)
        kr_f = rot(kr_ref[:, cols].astype(F32))
        kr = kr_f.astype(BF16)
        v = vr_ref[:, cols]
        sr = _dot_nt(qr, kr) * decay_ref[h]
        inner = jnp.dot(sr.astype(BF16), v, preferred_element_type=F32)
        st = state_ref[h]
        cross = jnp.dot(qr, st.astype(BF16), preferred_element_type=F32) * qdec_ref[h]
        kd = (kr_f * kdec_ref[h]).astype(BF16)
        state_ref[h] = st * math.exp(_LOG_GAMMA[h] * TQ) + _dot_tn(kd, v)
        out = inner + cross
        out = out * lax.rsqrt(jnp.mean(out * out, axis=-1, keepdims=True) + EPS)
        ret_ref[:, cols] = (_silu(gr_ref[:, cols].astype(F32)) * out).astype(BF16)


def _mixer(proj, cos2, sin2, rel_row, batch, seq):
    n_t = seq // TQ
    proj3 = proj.reshape(batch, seq, proj.shape[-1])

    def tile(col):
        return pl.BlockSpec((None, TQ, WIDTH), lambda b, i: (b, i, col))

    def hist(col, j):
        return pl.BlockSpec((None, TQ, WIDTH),
                            lambda b, i: (b, jnp.maximum(i - (N_KBLK - 1) + j, 0), col))

    in_specs = ([tile(0)] + [hist(1, j) for j in range(N_KBLK)] + [hist(2, j) for j in range(N_KBLK)]
                + [tile(3), tile(4), tile(5), tile(6)]
                + [pl.BlockSpec((TQ, HEAD_DIM), lambda b, i: (i, 0)),
                   pl.BlockSpec((TQ, HEAD_DIM), lambda b, i: (i, 0)),
                   pl.BlockSpec((HEADS, (N_KBLK + 1) * TQ), lambda b, i: (0, 0))])
    out_spec = pl.BlockSpec((None, TQ, WIDTH), lambda b, i: (b, i, 0))
    att, ret = pl.pallas_call(
        _mixer_kernel,
        grid=(batch, n_t),
        in_specs=in_specs,
        out_specs=[out_spec, out_spec],
        out_shape=[jax.ShapeDtypeStruct((batch, seq, WIDTH), BF16)] * 2,
        scratch_shapes=[pltpu.VMEM((HEADS + 1, TQ, N_KBLK * TQ), F32),
                        pltpu.VMEM((HEADS, TQ, TQ), F32),
                        pltpu.VMEM((HEADS, TQ, HEAD_DIM), F32),
                        pltpu.VMEM((HEADS, TQ, HEAD_DIM), F32),
                        pltpu.VMEM((HEADS, HEAD_DIM, HEAD_DIM), F32)],
        compiler_params=pltpu.CompilerParams(
            dimension_semantics=("arbitrary", "arbitrary"),
            vmem_limit_bytes=48 * MIB),
        name="mixer",
    )(*([proj3] * 11), cos2, sin2, rel_row)
    return att.reshape(batch * seq, WIDTH), ret.reshape(batch * seq, WIDTH)


def _out_proj_kernel(att_ref, ret_ref, ga0_ref, ga1_ref, gr0_ref, gr1_ref, x_ref,
                     woa_ref, wor_ref, wo_ref, o_ref, merged_ref):
    att = att_ref[...]
    ret = ret_ref[...]
    for c, (ga_ref, gr_ref) in enumerate(((ga0_ref, gr0_ref), (ga1_ref, gr1_ref))):
        cols = slice(c * WIDTH, (c + 1) * WIDTH)
        branch_a = jnp.dot(att, woa_ref[:, cols], preferred_element_type=F32)
        branch_r = jnp.dot(ret, wor_ref[:, cols], preferred_element_type=F32)
        merged = (jax.nn.sigmoid(ga_ref[...].astype(F32)) * branch_a
                  + jax.nn.sigmoid(gr_ref[...].astype(F32)) * branch_r)
        merged_ref[:, cols] = merged.astype(BF16)
    o_ref[...] = x_ref[...] + jnp.dot(merged_ref[...], wo_ref[...], preferred_element_type=F32)


def _out_proj(att, ret, proj, x, woa, wor, wo, *, tm=256):
    t, d = x.shape
    gate0 = 7

    def act(col):
        return pl.BlockSpec((tm, WIDTH), lambda i: (i, col))

    def whole(shape):
        return pl.BlockSpec(shape, lambda i: (0, 0))

    return pl.pallas_call(
        _out_proj_kernel,
        grid=(t // tm,),
        in_specs=[act(0), act(0), act(gate0), act(gate0 + 1), act(gate0 + 2), act(gate0 + 3),
                  pl.BlockSpec((tm, d), lambda i: (i, 0)),
                  whole(woa.shape), whole(wor.shape), whole(wo.shape)],
        out_specs=pl.BlockSpec((tm, d), lambda i: (i, 0)),
        out_shape=jax.ShapeDtypeStruct((t, d), F32),
        scratch_shapes=[pltpu.VMEM((tm, d), BF16)],
        compiler_params=pltpu.CompilerParams(
            dimension_semantics=("arbitrary",),
            vmem_limit_bytes=56 * MIB),
        name="out_proj",
    )(att, ret, proj, proj, proj, proj, x, woa, wor, wo)


def _rotary_tables(seq):
    inv = 1.0 / (ROPE_BASE ** (jnp.arange(0, HEAD_DIM, 2, dtype=F32) / HEAD_DIM))
    ang = jnp.arange(seq, dtype=F32)[:, None] * inv[None, :]
    cos = jnp.cos(ang)
    sin = jnp.sin(ang)
    return jnp.concatenate([cos, cos], axis=-1), jnp.concatenate([-sin, sin], axis=-1)


def _rel_row(rel_bias):
    j = np.arange((N_KBLK + 1) * TQ)
    rel_idx = np.clip(N_KBLK * TQ - j, -MAX_REL_DIST, MAX_REL_DIST) + MAX_REL_DIST
    return rel_bias.astype(F32)[:, rel_idx]


def kernel(x, norm_ffn1_g, ffn1_w_gate, ffn1_w_up, ffn1_w_down, norm_mix_g, w_in, rel_bias,
           w_out_att, w_out_ret, w_out, norm_ffn2_g, ffn2_w_gate, ffn2_w_up, ffn2_w_down,
           norm_final_g):
    batch, seq, d = x.shape
    depth = w_in.shape[0]
    assert depth >= 1 and (N_KBLK - 1) * TQ == N_PREV_CHUNKS * CHUNK and seq % TQ == 0
    cos2, sin2 = _rotary_tables(seq)
    xt = x.reshape(batch * seq, d)
    for l in range(depth):
        last = l == depth - 1
        xt, (w_in_b,) = _ffn(xt, norm_ffn1_g[l][None], ffn1_w_gate[l].astype(BF16),
                             ffn1_w_up[l].astype(BF16), ffn1_w_down[l].astype(BF16), cast=(w_in[l],))
        proj, (wg2, wu2, wd2, woa, wor, wo) = _in_proj(
            xt, norm_mix_g[l][None], w_in_b, first_tile_scale=HEAD_DIM ** -0.5 * LOG2E,
            cast=(ffn2_w_gate[l], ffn2_w_up[l], ffn2_w_down[l], w_out_att[l], w_out_ret[l], w_out[l]))
        att, ret = _mixer(proj, cos2, sin2, _rel_row(rel_bias[l]), batch, seq)
        xt = _out_proj(att, ret, proj, xt, woa, wor, wo)
        xt, _ = _ffn(xt, norm_ffn2_g[l][None], wg2, wu2, wd2, norm_final_g[None] if last else None)
    return xt.reshape(batch, seq, d)
```

```python
import functools
import math

import numpy as np
import jax
import jax.numpy as jnp
from jax import lax
from jax.experimental import pallas as pl
from jax.experimental.pallas import tpu as pltpu

F32 = jnp.float32
BF16 = jnp.bfloat16

CHUNK = 64
N_PREV_CHUNKS = 8
HEADS = 8
HEAD_DIM = 128
WIDTH = HEADS * HEAD_DIM
MAX_REL_DIST = 128
ROPE_BASE = 10000.0
EPS = 1e-6
NEG = -1e30
LOG2E = math.log2(math.e)

LANES = 128
BF16_SUBLANES = 16

TQ = 256
N_KBLK = 3
OUT_PIECE = 256
MIB = 1024 * 1024


def _rmsnorm(x, g):
    ms = jnp.mean(x * x, axis=-1, keepdims=True)
    return x * lax.rsqrt(ms + EPS) * g


def _silu(x):
    return x * jax.nn.sigmoid(x)


def _cast_plan(shape, n_i, n_j):
    r, c = shape
    if r % (n_i * BF16_SUBLANES) == 0 and c % (n_j * LANES) == 0:
        return pl.BlockSpec((r // n_i, c // n_j), lambda i, j: (i, j)), True
    if r % (n_j * BF16_SUBLANES) == 0 and c % (n_i * LANES) == 0:
        return pl.BlockSpec((r // n_j, c // n_i), lambda i, j: (j, i)), True
    assert r % (n_i * BF16_SUBLANES) == 0, shape
    return pl.BlockSpec((r // n_i, c), lambda i, j: (i, 0)), False


def _cast_side_args(weights, n_i, n_j):
    plans = [_cast_plan(w.shape, n_i, n_j) for w in weights]
    order = ([k for k, (_, every) in enumerate(plans) if every]
             + [k for k, (_, every) in enumerate(plans) if not every])
    specs = [plans[k][0] for k in order]
    n_every = sum(1 for _, every in plans if every)
    return order, specs, n_every


def _unpermute(values, order):
    out = [None] * len(order)
    for pos, k in enumerate(order):
        out[k] = values[pos]
    return out


def _cast_blocks(pairs):
    for src_ref, dst_ref in pairs:
        dst_ref[...] = src_ref[...].astype(BF16)


def _ffn_kernel(*refs, n_f, final_norm, n_cast, n_cast_every):
    n_in = 6 if final_norm else 5
    x_ref, g_ref, wg_ref, wu_ref, wd_ref = refs[:5]
    gf_ref = refs[5] if final_norm else None
    o_ref = refs[n_in + n_cast]
    h_ref = refs[-1]
    casts = list(zip(refs[n_in:n_in + n_cast], refs[n_in + n_cast + 1:n_in + 2 * n_cast + 1]))
    cast_every, cast_first = casts[:n_cast_every], casts[n_cast_every:]
    j = pl.program_id(1)

    def partial_down(h):
        gate = jnp.dot(h, wg_ref[...], preferred_element_type=F32)
        up = jnp.dot(h, wu_ref[...], preferred_element_type=F32)
        act = (_silu(gate) * up).astype(BF16)
        return jnp.dot(act, wd_ref[...], preferred_element_type=F32)

    @pl.when(j == 0)
    def _():
        _cast_blocks(cast_every + cast_first)
        h = _rmsnorm(x_ref[...], g_ref[...]).astype(BF16)
        h_ref[...] = h
        o_ref[...] = partial_down(h)

    @pl.when(j > 0)
    def _():
        _cast_blocks(cast_every)
        o_ref[...] += partial_down(h_ref[...])

    @pl.when(j == n_f - 1)
    def _():
        y = x_ref[...] + 0.5 * o_ref[...]
        if final_norm:
            y = _rmsnorm(y, gf_ref[...])
        o_ref[...] = y


def _ffn(x, g, wg, wu, wd, g_final=None, *, cast=(), tm=1024, tf=512):
    t, d = x.shape
    f = wg.shape[1]
    n_i, n_f = t // tm, f // tf
    final_norm = g_final is not None
    in_specs = [
        pl.BlockSpec((tm, d), lambda i, j: (i, 0)),
        pl.BlockSpec((1, d), lambda i, j: (0, 0)),
        pl.BlockSpec((d, tf), lambda i, j: (0, j)),
        pl.BlockSpec((d, tf), lambda i, j: (0, j)),
        pl.BlockSpec((tf, d), lambda i, j: (j, 0)),
    ]
    args = [x, g, wg, wu, wd]
    if final_norm:
        in_specs.append(pl.BlockSpec((1, d), lambda i, j: (0, 0)))
        args.append(g_final)
    order, cast_specs, n_every = _cast_side_args(cast, n_i, n_f)
    outs = pl.pallas_call(
        functools.partial(_ffn_kernel, n_f=n_f, final_norm=final_norm, n_cast=len(cast),
                          n_cast_every=n_every),
        grid=(n_i, n_f),
        in_specs=in_specs + cast_specs,
        out_specs=[pl.BlockSpec((tm, d), lambda i, j: (i, 0))] + cast_specs,
        out_shape=[jax.ShapeDtypeStruct((t, d), F32)]
        + [jax.ShapeDtypeStruct(cast[k].shape, BF16) for k in order],
        scratch_shapes=[pltpu.VMEM((tm, d), BF16)],
        compiler_params=pltpu.CompilerParams(
            dimension_semantics=("arbitrary", "arbitrary"),
            vmem_limit_bytes=60 * MIB),
        name="ffn_final" if final_norm else "ffn",
    )(*args, *[cast[k] for k in order])
    return outs[0], _unpermute(outs[1:], order)


def _in_proj_kernel(*refs, first_tile_scale, n_cast, n_cast_every):
    x_ref, g_ref, w_ref = refs[:3]
    o_ref = refs[3 + n_cast]
    h_ref = refs[-1]
    casts = list(zip(refs[3:3 + n_cast], refs[4 + n_cast:4 + 2 * n_cast]))
    cast_every, cast_first = casts[:n_cast_every], casts[n_cast_every:]
    j = pl.program_id(1)

    def project(h):
        return jnp.dot(h, w_ref[...], preferred_element_type=F32)

    @pl.when(j == 0)
    def _():
        _cast_blocks(cast_every + cast_first)
        h = _rmsnorm(x_ref[...], g_ref[...]).astype(BF16)
        h_ref[...] = h
        o_ref[...] = (project(h) * first_tile_scale).astype(BF16)

    @pl.when(j > 0)
    def _():
        _cast_blocks(cast_every)
        o_ref[...] = project(h_ref[...]).astype(BF16)


def _in_proj(x, g, w, *, first_tile_scale, cast=(), tm=1024, tn=WIDTH):
    t, d = x.shape
    n = w.shape[1]
    n_i, n_j = t // tm, n // tn
    order, cast_specs, n_every = _cast_side_args(cast, n_i, n_j)
    outs = pl.pallas_call(
        functools.partial(_in_proj_kernel, first_tile_scale=first_tile_scale, n_cast=len(cast),
                          n_cast_every=n_every),
        grid=(n_i, n_j),
        in_specs=[
            pl.BlockSpec((tm, d), lambda i, j: (i, 0)),
            pl.BlockSpec((1, d), lambda i, j: (0, 0)),
            pl.BlockSpec((d, tn), lambda i, j: (0, j)),
        ] + cast_specs,
        out_specs=[pl.BlockSpec((tm, tn), lambda i, j: (i, j))] + cast_specs,
        out_shape=[jax.ShapeDtypeStruct((t, n), BF16)]
        + [jax.ShapeDtypeStruct(cast[k].shape, BF16) for k in order],
        scratch_shapes=[pltpu.VMEM((tm, d), BF16)],
        compiler_params=pltpu.CompilerParams(
            dimension_semantics=("arbitrary", "arbitrary"),
            vmem_limit_bytes=48 * MIB),
        name="in_proj",
    )(x, g, w, *[cast[k] for k in order])
    return outs[0], _unpermute(outs[1:], order)


_LOG_GAMMA = tuple(math.log(1.0 - 2.0 ** (-5.0 - h)) for h in range(HEADS))


def _dot_nt(a, b):
    return lax.dot_general(a, b, (((1,), (1,)), ((), ())), preferred_element_type=F32)


def _dot_tn(a, b):
    return lax.dot_general(a, b, (((0,), (0,)), ((), ())), preferred_element_type=F32)


def _mixer_out_kernel(qa_ref, ka0_ref, ka1_ref, ka2_ref, va0_ref, va1_ref, va2_ref,
                      qr_ref, kr_ref, vr_ref, gr_ref, ga0_ref, ga1_ref, gm0_ref, gm1_ref,
                      cos_ref, sin_ref, relrow_ref, xprev_ref, woa_ref, wor_ref, wo_ref,
                      o_ref,
                      bias_ref, decay_ref, state_ref,
                      att_ref, ret_ref, attp_ref, retp_ref, merged_ref, *, n_tiles, tiles_per_seq):
    s = pl.program_id(0)
    t = jnp.minimum(s, n_tiles - 1)
    i = lax.rem(t, tiles_per_seq)
    ka_refs = (ka0_ref, ka1_ref, ka2_ref)
    va_refs = (va0_ref, va1_ref, va2_ref)

    @pl.when(s == 0)
    def _():
        qc = lax.broadcasted_iota(jnp.int32, (TQ, N_KBLK * TQ), 0) >> 6
        kc = lax.broadcasted_iota(jnp.int32, (TQ, N_KBLK * TQ), 1) >> 6
        in_band = (kc >= qc) & (kc <= qc + N_PREV_CHUNKS)
        for h in range(HEADS):
            rows = jnp.broadcast_to(relrow_ref[h:h + 1, :], (TQ, (N_KBLK + 1) * TQ))
            toeplitz = pltpu.roll(rows, 0, axis=1, stride=1, stride_axis=0)
            bias_ref[h] = jnp.where(in_band, toeplitz[:, TQ:] * LOG2E, NEG)
        bias_ref[HEADS] = jnp.full((TQ, N_KBLK * TQ), NEG, F32)

        n = lax.broadcasted_iota(jnp.int32, (TQ, TQ), 0)
        m = lax.broadcasted_iota(jnp.int32, (TQ, TQ), 1)
        dist = jnp.abs(n - m).astype(F32)
        visible = (m >> 6) <= (n >> 6)
        for h in range(HEADS):
            decay_ref[h] = jnp.where(visible, jnp.exp(_LOG_GAMMA[h] * dist), 0.0)
        att_ref[...] = jnp.zeros_like(att_ref)
        ret_ref[...] = jnp.zeros_like(ret_ref)

    @pl.when(i == 0)
    def _():
        state_ref[...] = jnp.zeros_like(state_ref)

    attp_ref[...] = att_ref[...]
    retp_ref[...] = ret_ref[...]

    ret_scale = HEAD_DIM ** -0.5
    cos = cos_ref[...]
    sin = sin_ref[...]
    row = lax.broadcasted_iota(jnp.int32, (TQ, HEAD_DIM), 0).astype(F32)

    def rot(x):
        return x * cos + pltpu.roll(x, HEAD_DIM // 2, axis=1) * sin

    def head(h, fill_a, fill_b):
        cols = slice(h * HEAD_DIM, (h + 1) * HEAD_DIM)

        q = qa_ref[:, cols]
        sc = []
        for j in range(N_KBLK):
            plane = h if j == N_KBLK - 1 else jnp.where(i + j >= N_KBLK - 1, h, HEADS)
            sc.append(_dot_nt(q, ka_refs[j][:, cols]) + bias_ref[plane, :, j * TQ:(j + 1) * TQ])
        fill_a()
        mx = jnp.maximum(jnp.maximum(jnp.max(sc[0], axis=-1, keepdims=True),
                                     jnp.max(sc[1], axis=-1, keepdims=True)),
                         jnp.max(sc[2], axis=-1, keepdims=True))
        acc = None
        den = None
        for j in range(N_KBLK):
            p = jnp.exp2(sc[j] - mx)
            dj = jnp.sum(p, axis=-1, keepdims=True)
            oj = jnp.dot(p.astype(BF16), va_refs[j][:, cols], preferred_element_type=F32)
            acc = oj if acc is None else acc + oj
            den = dj if den is None else den + dj
        att_ref[:, cols] = (acc / den).astype(BF16)

        qr = (rot(qr_ref[:, cols].astype(F32)) * ret_scale).astype(BF16)
        kr_f = rot(kr_ref[:, cols].astype(F32))
        kr = kr_f.astype(BF16)
        v = vr_ref[:, cols]
        sr = _dot_nt(qr, kr)
        st = state_ref[h]
        lg = _LOG_GAMMA[h]
        cross = jnp.dot(qr, st.astype(BF16), preferred_element_type=F32)
        fill_b()
        inner = jnp.dot((sr * decay_ref[h]).astype(BF16), v, preferred_element_type=F32)
        kd = (kr_f * jnp.exp(lg * (TQ - 1.0 - row))).astype(BF16)
        state_ref[h] = st * math.exp(lg * TQ) + _dot_tn(kd, v)
        out = inner + cross * jnp.exp(lg * (row + 1.0))
        out = out * lax.rsqrt(jnp.mean(out * out, axis=-1, keepdims=True) + EPS)
        ret_ref[:, cols] = (_silu(gr_ref[:, cols].astype(F32)) * out).astype(BF16)

    gate_refs = ((ga0_ref, gm0_ref), (ga1_ref, gm1_ref))

    def merge_piece(k):
        cols = slice(k * OUT_PIECE, (k + 1) * OUT_PIECE)
        ga_ref, gm_ref = gate_refs[k * OUT_PIECE // WIDTH]
        gcols = slice(k * OUT_PIECE % WIDTH, k * OUT_PIECE % WIDTH + OUT_PIECE)
        branch_a = jnp.dot(attp_ref[...], woa_ref[:, cols], preferred_element_type=F32)
        branch_r = jnp.dot(retp_ref[...], wor_ref[:, cols], preferred_element_type=F32)
        merged = (jax.nn.sigmoid(ga_ref[:, gcols].astype(F32)) * branch_a
                  + jax.nn.sigmoid(gm_ref[:, gcols].astype(F32)) * branch_r)
        merged_ref[:, cols] = merged.astype(BF16)

    def out_piece(k):
        cols = slice(k * OUT_PIECE, (k + 1) * OUT_PIECE)
        o_ref[:, cols] = xprev_ref[:, cols] + jnp.dot(merged_ref[...], wo_ref[:, cols],
                                                      preferred_element_type=F32)

    n_pieces = merged_ref.shape[1] // OUT_PIECE
    pieces = ([functools.partial(merge_piece, k) for k in range(n_pieces)]
              + [functools.partial(out_piece, k) for k in range(n_pieces)])
    assert len(pieces) == 2 * HEADS
    for h in range(HEADS):
        head(h, pieces[2 * h], pieces[2 * h + 1])


def _mixer_out(proj, x, cos2, sin2, rel_row, woa, wor, wo, batch, seq):
    t, d = x.shape
    tiles_per_seq = seq // TQ
    n_tiles = batch * tiles_per_seq
    proj3 = proj.reshape(batch, seq, proj.shape[-1])

    def split(tt):
        return tt // tiles_per_seq, lax.rem(tt, tiles_per_seq)

    def cur(s):
        return split(jnp.minimum(s, n_tiles - 1))

    def tile(col):
        return pl.BlockSpec((None, TQ, WIDTH), lambda s: (*cur(s), col))

    def prev_tile(col):
        return pl.BlockSpec((None, TQ, WIDTH), lambda s: (*split(jnp.maximum(s - 1, 0)), col))

    def hist(col, j):
        def index(s):
            b, i = cur(s)
            return b, jnp.maximum(i - (N_KBLK - 1) + j, 0), col
        return pl.BlockSpec((None, TQ, WIDTH), index)

    def table():
        return pl.BlockSpec((TQ, HEAD_DIM), lambda s: (cur(s)[1], 0))

    def resident(shape):
        return pl.BlockSpec(shape, lambda s: (0, 0), pipeline_mode=pl.Buffered(1))

    prev = pl.BlockSpec((TQ, d), lambda s: (jnp.maximum(s - 1, 0), 0))
    gate0 = 7
    in_specs = ([tile(0)] + [hist(1, j) for j in range(N_KBLK)] + [hist(2, j) for j in range(N_KBLK)]
                + [tile(3), tile(4), tile(5), tile(6)]
                + [prev_tile(gate0 + c) for c in range(4)]
                + [table(), table(), resident(rel_row.shape), prev,
                   resident(woa.shape), resident(wor.shape), resident(wo.shape)])
    return pl.pallas_call(
        functools.partial(_mixer_out_kernel, n_tiles=n_tiles, tiles_per_seq=tiles_per_seq),
        grid=(n_tiles + 1,),
        in_specs=in_specs,
        out_specs=prev,
        out_shape=jax.ShapeDtypeStruct((t, d), F32),
        scratch_shapes=[pltpu.VMEM((HEADS + 1, TQ, N_KBLK * TQ), F32),
                        pltpu.VMEM((HEADS, TQ, TQ), F32),
                        pltpu.VMEM((HEADS, HEAD_DIM, HEAD_DIM), F32),
                        pltpu.VMEM((TQ, WIDTH), BF16),
                        pltpu.VMEM((TQ, WIDTH), BF16),
                        pltpu.VMEM((TQ, WIDTH), BF16),
                        pltpu.VMEM((TQ, WIDTH), BF16),
                        pltpu.VMEM((TQ, d), BF16)],
        compiler_params=pltpu.CompilerParams(
            dimension_semantics=("arbitrary",),
            vmem_limit_bytes=60 * MIB),
        name="mixer_out",
    )(*([proj3] * 15), cos2, sin2, rel_row, x, woa, wor, wo)


def _rotary_tables(seq):
    inv = 1.0 / (ROPE_BASE ** (jnp.arange(0, HEAD_DIM, 2, dtype=F32) / HEAD_DIM))
    ang = jnp.arange(seq, dtype=F32)[:, None] * inv[None, :]
    cos = jnp.cos(ang)
    sin = jnp.sin(ang)
    return jnp.concatenate([cos, cos], axis=-1), jnp.concatenate([-sin, sin], axis=-1)


def _rel_row(rel_bias):
    j = np.arange((N_KBLK + 1) * TQ)
    rel_idx = np.clip(N_KBLK * TQ - j, -MAX_REL_DIST, MAX_REL_DIST) + MAX_REL_DIST
    return rel_bias.astype(F32)[:, rel_idx]


def kernel(x, norm_ffn1_g, ffn1_w_gate, ffn1_w_up, ffn1_w_down, norm_mix_g, w_in, rel_bias,
           w_out_att, w_out_ret, w_out, norm_ffn2_g, ffn2_w_gate, ffn2_w_up, ffn2_w_down,
           norm_final_g):
    batch, seq, d = x.shape
    depth = w_in.shape[0]
    assert depth >= 1 and (N_KBLK - 1) * TQ == N_PREV_CHUNKS * CHUNK and seq % TQ == 0
    cos2, sin2 = _rotary_tables(seq)
    xt = x.reshape(batch * seq, d)
    for l in range(depth):
        last = l == depth - 1
        xt, (w_in_b,) = _ffn(xt, norm_ffn1_g[l][None], ffn1_w_gate[l].astype(BF16),
                             ffn1_w_up[l].astype(BF16), ffn1_w_down[l].astype(BF16), cast=(w_in[l],))
        proj, (wg2, wu2, wd2, woa, wor, wo) = _in_proj(
            xt, norm_mix_g[l][None], w_in_b, first_tile_scale=HEAD_DIM ** -0.5 * LOG2E,
            cast=(ffn2_w_gate[l], ffn2_w_up[l], ffn2_w_down[l], w_out_att[l], w_out_ret[l], w_out[l]))
        xt = _mixer_out(proj, xt, cos2, sin2, _rel_row(rel_bias[l]), woa, wor, wo, batch, seq)
        xt, _ = _ffn(xt, norm_ffn2_g[l][None], wg2, wu2, wd2, norm_final_g[None] if last else None)
    return xt.reshape(batch, seq, d)
```

```python
import functools
import math

import numpy as np
import jax
import jax.numpy as jnp
from jax import lax
from jax.experimental import pallas as pl
from jax.experimental.pallas import tpu as pltpu

F32 = jnp.float32
BF16 = jnp.bfloat16

CHUNK = 64
N_PREV_CHUNKS = 8
HEADS = 8
HEAD_DIM = 128
WIDTH = HEADS * HEAD_DIM
MAX_REL_DIST = 128
ROPE_BASE = 10000.0
EPS = 1e-6
NEG = -1e30
LOG2E = math.log2(math.e)

LANES = 128
BF16_SUBLANES = 16

TQ = 256
N_KBLK = 3
OUT_PIECE = 256
MIB = 1024 * 1024


def _rmsnorm(x, g):
    ms = jnp.mean(x * x, axis=-1, keepdims=True)
    return x * lax.rsqrt(ms + EPS) * g


def _silu(x):
    return x * jax.nn.sigmoid(x)


def _cast_plan(shape, n_i, n_j):
    r, c = shape
    if r % (n_i * n_j * BF16_SUBLANES) == 0:
        return pl.BlockSpec((r // (n_i * n_j), c), lambda i, j: (i * n_j + j, 0)), True
    if r % (n_i * BF16_SUBLANES) == 0 and c % (n_j * LANES) == 0:
        return pl.BlockSpec((r // n_i, c // n_j), lambda i, j: (i, j)), True
    if r % (n_j * BF16_SUBLANES) == 0 and c % (n_i * LANES) == 0:
        return pl.BlockSpec((r // n_j, c // n_i), lambda i, j: (j, i)), True
    assert r % (n_i * BF16_SUBLANES) == 0, shape
    return pl.BlockSpec((r // n_i, c), lambda i, j: (i, 0)), False


def _cast_side_args(weights, n_i, n_j):
    plans = [_cast_plan(w.shape, n_i, n_j) for w in weights]
    order = ([k for k, (_, every) in enumerate(plans) if every]
             + [k for k, (_, every) in enumerate(plans) if not every])
    specs = [plans[k][0] for k in order]
    n_every = sum(1 for _, every in plans if every)
    return order, specs, n_every


def _unpermute(values, order):
    out = [None] * len(order)
    for pos, k in enumerate(order):
        out[k] = values[pos]
    return out


def _cast_blocks(pairs):
    for src_ref, dst_ref in pairs:
        dst_ref[...] = src_ref[...].astype(BF16)


def _ffn_kernel(*refs, n_f, final_norm, n_cast, n_cast_every):
    n_in = 6 if final_norm else 5
    x_ref, g_ref, wg_ref, wu_ref, wd_ref = refs[:5]
    gf_ref = refs[5] if final_norm else None
    o_ref = refs[n_in + n_cast]
    h_ref = refs[-1]
    casts = list(zip(refs[n_in:n_in + n_cast], refs[n_in + n_cast + 1:n_in + 2 * n_cast + 1]))
    cast_every, cast_first = casts[:n_cast_every], casts[n_cast_every:]
    j = pl.program_id(1)

    def partial_down(h):
        gate = jnp.dot(h, wg_ref[...], preferred_element_type=F32)
        up = jnp.dot(h, wu_ref[...], preferred_element_type=F32)
        act = (_silu(gate) * up).astype(BF16)
        return jnp.dot(act, wd_ref[...], preferred_element_type=F32)

    @pl.when(j == 0)
    def _():
        _cast_blocks(cast_every + cast_first)
        h = _rmsnorm(x_ref[...], g_ref[...]).astype(BF16)
        h_ref[...] = h
        o_ref[...] = partial_down(h)

    @pl.when((j > 0) & (j < n_f - 1))
    def _():
        _cast_blocks(cast_every)
        o_ref[...] += partial_down(h_ref[...])

    @pl.when(j == n_f - 1)
    def _():
        _cast_blocks(cast_every)
        y = x_ref[...] + 0.5 * (o_ref[...] + partial_down(h_ref[...]))
        if final_norm:
            y = _rmsnorm(y, gf_ref[...])
        o_ref[...] = y


def _ffn(x, g, wg, wu, wd, g_final=None, *, cast=(), tm=1024, tf=512):
    t, d = x.shape
    f = wg.shape[1]
    n_i, n_f = t // tm, f // tf
    assert n_f >= 2
    final_norm = g_final is not None
    in_specs = [
        pl.BlockSpec((tm, d), lambda i, j: (i, 0)),
        pl.BlockSpec((1, d), lambda i, j: (0, 0)),
        pl.BlockSpec((d, tf), lambda i, j: (0, j)),
        pl.BlockSpec((d, tf), lambda i, j: (0, j)),
        pl.BlockSpec((tf, d), lambda i, j: (j, 0)),
    ]
    args = [x, g, wg, wu, wd]
    if final_norm:
        in_specs.append(pl.BlockSpec((1, d), lambda i, j: (0, 0)))
        args.append(g_final)
    order, cast_specs, n_every = _cast_side_args(cast, n_i, n_f)
    outs = pl.pallas_call(
        functools.partial(_ffn_kernel, n_f=n_f, final_norm=final_norm, n_cast=len(cast),
                          n_cast_every=n_every),
        grid=(n_i, n_f),
        in_specs=in_specs + cast_specs,
        out_specs=[pl.BlockSpec((tm, d), lambda i, j: (i, 0))] + cast_specs,
        out_shape=[jax.ShapeDtypeStruct((t, d), F32)]
        + [jax.ShapeDtypeStruct(cast[k].shape, BF16) for k in order],
        scratch_shapes=[pltpu.VMEM((tm, d), BF16)],
        compiler_params=pltpu.CompilerParams(
            dimension_semantics=("arbitrary", "arbitrary"),
            vmem_limit_bytes=60 * MIB),
        name="ffn_final" if final_norm else "ffn",
    )(*args, *[cast[k] for k in order])
    return outs[0], _unpermute(outs[1:], order)


def _in_proj_kernel(*refs, first_tile_scale, n_cast, n_cast_every):
    x_ref, g_ref, w_ref = refs[:3]
    o_ref = refs[3 + n_cast]
    h_ref = refs[-1]
    casts = list(zip(refs[3:3 + n_cast], refs[4 + n_cast:4 + 2 * n_cast]))
    cast_every, cast_first = casts[:n_cast_every], casts[n_cast_every:]
    j = pl.program_id(1)

    def project(h):
        return jnp.dot(h, w_ref[...], preferred_element_type=F32)

    @pl.when(j == 0)
    def _():
        _cast_blocks(cast_every + cast_first)
        h = _rmsnorm(x_ref[...], g_ref[...]).astype(BF16)
        h_ref[...] = h
        o_ref[...] = (project(h) * first_tile_scale).astype(BF16)

    @pl.when(j > 0)
    def _():
        _cast_blocks(cast_every)
        o_ref[...] = project(h_ref[...]).astype(BF16)


def _in_proj(x, g, w, *, first_tile_scale, cast=(), tm=1024, tn=WIDTH):
    t, d = x.shape
    n = w.shape[1]
    n_i, n_j = t // tm, n // tn
    order, cast_specs, n_every = _cast_side_args(cast, n_i, n_j)
    outs = pl.pallas_call(
        functools.partial(_in_proj_kernel, first_tile_scale=first_tile_scale, n_cast=len(cast),
                          n_cast_every=n_every),
        grid=(n_i, n_j),
        in_specs=[
            pl.BlockSpec((tm, d), lambda i, j: (i, 0)),
            pl.BlockSpec((1, d), lambda i, j: (0, 0)),
            pl.BlockSpec((d, tn), lambda i, j: (0, j)),
        ] + cast_specs,
        out_specs=[pl.BlockSpec((tm, tn), lambda i, j: (i, j))] + cast_specs,
        out_shape=[jax.ShapeDtypeStruct((t, n), BF16)]
        + [jax.ShapeDtypeStruct(cast[k].shape, BF16) for k in order],
        scratch_shapes=[pltpu.VMEM((tm, d), BF16)],
        compiler_params=pltpu.CompilerParams(
            dimension_semantics=("arbitrary", "arbitrary"),
            vmem_limit_bytes=48 * MIB),
        name="in_proj",
    )(x, g, w, *[cast[k] for k in order])
    return outs[0], _unpermute(outs[1:], order)


_LOG_GAMMA = tuple(math.log(1.0 - 2.0 ** (-5.0 - h)) for h in range(HEADS))


def _dot_nt(a, b):
    return lax.dot_general(a, b, (((1,), (1,)), ((), ())), preferred_element_type=F32)


def _dot_tn(a, b):
    return lax.dot_general(a, b, (((0,), (0,)), ((), ())), preferred_element_type=F32)


def _mixer_out_kernel(qa_ref, ka0_ref, ka1_ref, ka2_ref, va0_ref, va1_ref, va2_ref,
                      qr_ref, kr_ref, vr_ref, gr_ref, ga0_ref, ga1_ref, gm0_ref, gm1_ref,
                      cos_ref, sin_ref, relrow_ref, xprev_ref, woa_ref, wor_ref, wo_ref,
                      o_ref,
                      bias_ref, decay_ref, state_ref,
                      att_ref, ret_ref, attp_ref, retp_ref, merged_ref, *, n_tiles, tiles_per_seq):
    s = pl.program_id(0)
    t = jnp.minimum(s, n_tiles - 1)
    i = lax.rem(t, tiles_per_seq)
    ka_refs = (ka0_ref, ka1_ref, ka2_ref)
    va_refs = (va0_ref, va1_ref, va2_ref)

    @pl.when(s == 0)
    def _():
        qc = lax.broadcasted_iota(jnp.int32, (TQ, N_KBLK * TQ), 0) >> 6
        kc = lax.broadcasted_iota(jnp.int32, (TQ, N_KBLK * TQ), 1) >> 6
        in_band = (kc >= qc) & (kc <= qc + N_PREV_CHUNKS)
        for h in range(HEADS):
            rows = jnp.broadcast_to(relrow_ref[h:h + 1, :], (TQ, (N_KBLK + 1) * TQ))
            toeplitz = pltpu.roll(rows, 0, axis=1, stride=1, stride_axis=0)
            bias_ref[h] = jnp.where(in_band, toeplitz[:, TQ:] * LOG2E, NEG)
        bias_ref[HEADS] = jnp.full((TQ, N_KBLK * TQ), NEG, F32)

        n = lax.broadcasted_iota(jnp.int32, (TQ, TQ), 0)
        m = lax.broadcasted_iota(jnp.int32, (TQ, TQ), 1)
        dist = jnp.abs(n - m).astype(F32)
        visible = (m >> 6) <= (n >> 6)
        for h in range(HEADS):
            decay_ref[h] = jnp.where(visible, jnp.exp(_LOG_GAMMA[h] * dist), 0.0)
        att_ref[...] = jnp.zeros_like(att_ref)
        ret_ref[...] = jnp.zeros_like(ret_ref)

    @pl.when(i == 0)
    def _():
        state_ref[...] = jnp.zeros_like(state_ref)

    attp_ref[...] = att_ref[...]
    retp_ref[...] = ret_ref[...]

    ret_scale = HEAD_DIM ** -0.5
    cos = cos_ref[...]
    sin = sin_ref[...]
    row = lax.broadcasted_iota(jnp.int32, (TQ, HEAD_DIM), 0).astype(F32)

    def rot(x):
        return x * cos + pltpu.roll(x, HEAD_DIM // 2, axis=1) * sin

    def scores(h):
        cols = slice(h * HEAD_DIM, (h + 1) * HEAD_DIM)
        q = qa_ref[:, cols]
        sc = []
        for j in range(N_KBLK):
            plane = h if j == N_KBLK - 1 else jnp.where(i + j >= N_KBLK - 1, h, HEADS)
            sc.append(_dot_nt(q, ka_refs[j][:, cols]) + bias_ref[plane, :, j * TQ:(j + 1) * TQ])
        return sc

    def attend(h, sc):
        cols = slice(h * HEAD_DIM, (h + 1) * HEAD_DIM)
        mx = jnp.maximum(jnp.maximum(jnp.max(sc[0], axis=-1, keepdims=True),
                                     jnp.max(sc[1], axis=-1, keepdims=True)),
                         jnp.max(sc[2], axis=-1, keepdims=True))
        acc = None
        den = None
        for j in range(N_KBLK):
            p = jnp.exp2(sc[j] - mx)
            dj = jnp.sum(p, axis=-1, keepdims=True)
            oj = jnp.dot(p.astype(BF16), va_refs[j][:, cols], preferred_element_type=F32)
            acc = oj if acc is None else acc + oj
            den = dj if den is None else den + dj
        att_ref[:, cols] = (acc / den).astype(BF16)

    def retain(h, fill):
        cols = slice(h * HEAD_DIM, (h + 1) * HEAD_DIM)
        qr = (rot(qr_ref[:, cols].astype(F32)) * ret_scale).astype(BF16)
        kr_f = rot(kr_ref[:, cols].astype(F32))
        kr = kr_f.astype(BF16)
        v = vr_ref[:, cols]
        sr = _dot_nt(qr, kr)
        st = state_ref[h]
        lg = _LOG_GAMMA[h]
        cross = jnp.dot(qr, st.astype(BF16), preferred_element_type=F32)
        fill()
        inner = jnp.dot((sr * decay_ref[h]).astype(BF16), v, preferred_element_type=F32)
        kd = (kr_f * jnp.exp(lg * (TQ - 1.0 - row))).astype(BF16)
        state_ref[h] = st * math.exp(lg * TQ) + _dot_tn(kd, v)
        out = inner + cross * jnp.exp(lg * (row + 1.0))
        out = out * lax.rsqrt(jnp.mean(out * out, axis=-1, keepdims=True) + EPS)
        ret_ref[:, cols] = (_silu(gr_ref[:, cols].astype(F32)) * out).astype(BF16)

    gate_refs = ((ga0_ref, gm0_ref), (ga1_ref, gm1_ref))

    def merge_piece(k):
        cols = slice(k * OUT_PIECE, (k + 1) * OUT_PIECE)
        ga_ref, gm_ref = gate_refs[k * OUT_PIECE // WIDTH]
        gcols = slice(k * OUT_PIECE % WIDTH, k * OUT_PIECE % WIDTH + OUT_PIECE)
        branch_a = jnp.dot(attp_ref[...], woa_ref[:, cols], preferred_element_type=F32)
        branch_r = jnp.dot(retp_ref[...], wor_ref[:, cols], preferred_element_type=F32)
        merged = (jax.nn.sigmoid(ga_ref[:, gcols].astype(F32)) * branch_a
                  + jax.nn.sigmoid(gm_ref[:, gcols].astype(F32)) * branch_r)
        merged_ref[:, cols] = merged.astype(BF16)

    def out_piece(k):
        cols = slice(k * OUT_PIECE, (k + 1) * OUT_PIECE)
        o_ref[:, cols] = xprev_ref[:, cols] + jnp.dot(merged_ref[...], wo_ref[:, cols],
                                                      preferred_element_type=F32)

    n_pieces = merged_ref.shape[1] // OUT_PIECE
    pieces = ([functools.partial(merge_piece, k) for k in range(n_pieces)]
              + [functools.partial(out_piece, k) for k in range(n_pieces)])
    assert len(pieces) == 2 * HEADS
    sc = scores(0)
    for h in range(HEADS):
        sc_next = scores(h + 1) if h + 1 < HEADS else None
        pieces[2 * h]()
        attend(h, sc)
        retain(h, pieces[2 * h + 1])
        sc = sc_next


def _mixer_out(proj, x, cos2, sin2, rel_row, woa, wor, wo, batch, seq):
    t, d = x.shape
    tiles_per_seq = seq // TQ
    n_tiles = batch * tiles_per_seq
    proj3 = proj.reshape(batch, seq, proj.shape[-1])

    def split(tt):
        return tt // tiles_per_seq, lax.rem(tt, tiles_per_seq)

    def cur(s):
        return split(jnp.minimum(s, n_tiles - 1))

    def tile(col):
        return pl.BlockSpec((None, TQ, WIDTH), lambda s: (*cur(s), col))

    def prev_tile(col):
        return pl.BlockSpec((None, TQ, WIDTH), lambda s: (*split(jnp.maximum(s - 1, 0)), col))

    def hist(col, j):
        def index(s):
            b, i = cur(s)
            return b, jnp.maximum(i - (N_KBLK - 1) + j, 0), col
        return pl.BlockSpec((None, TQ, WIDTH), index)

    def table():
        return pl.BlockSpec((TQ, HEAD_DIM), lambda s: (cur(s)[1], 0))

    def resident(shape):
        return pl.BlockSpec(shape, lambda s: (0, 0), pipeline_mode=pl.Buffered(1))

    prev = pl.BlockSpec((TQ, d), lambda s: (jnp.maximum(s - 1, 0), 0))
    gate0 = 7
    in_specs = ([tile(0)] + [hist(1, j) for j in range(N_KBLK)] + [hist(2, j) for j in range(N_KBLK)]
                + [tile(3), tile(4), tile(5), tile(6)]
                + [prev_tile(gate0 + c) for c in range(4)]
                + [table(), table(), resident(rel_row.shape), prev,
                   resident(woa.shape), resident(wor.shape), resident(wo.shape)])
    return pl.pallas_call(
        functools.partial(_mixer_out_kernel, n_tiles=n_tiles, tiles_per_seq=tiles_per_seq),
        grid=(n_tiles + 1,),
        in_specs=in_specs,
        out_specs=prev,
        out_shape=jax.ShapeDtypeStruct((t, d), F32),
        scratch_shapes=[pltpu.VMEM((HEADS + 1, TQ, N_KBLK * TQ), F32),
                        pltpu.VMEM((HEADS, TQ, TQ), F32),
                        pltpu.VMEM((HEADS, HEAD_DIM, HEAD_DIM), F32),
                        pltpu.VMEM((TQ, WIDTH), BF16),
                        pltpu.VMEM((TQ, WIDTH), BF16),
                        pltpu.VMEM((TQ, WIDTH), BF16),
                        pltpu.VMEM((TQ, WIDTH), BF16),
                        pltpu.VMEM((TQ, d), BF16)],
        compiler_params=pltpu.CompilerParams(
            dimension_semantics=("arbitrary",),
            vmem_limit_bytes=60 * MIB),
        name="mixer_out",
    )(*([proj3] * 15), cos2, sin2, rel_row, x, woa, wor, wo)


def _rotary_tables(seq):
    inv = 1.0 / (ROPE_BASE ** (jnp.arange(0, HEAD_DIM, 2, dtype=F32) / HEAD_DIM))
    ang = jnp.arange(seq, dtype=F32)[:, None] * inv[None, :]
    cos = jnp.cos(ang)
    sin = jnp.sin(ang)
    return jnp.concatenate([cos, cos], axis=-1), jnp.concatenate([-sin, sin], axis=-1)


def _rel_row(rel_bias):
    j = np.arange((N_KBLK + 1) * TQ)
    rel_idx = np.clip(N_KBLK * TQ - j, -MAX_REL_DIST, MAX_REL_DIST) + MAX_REL_DIST
    return rel_bias.astype(F32)[:, rel_idx]


def kernel(x, norm_ffn1_g, ffn1_w_gate, ffn1_w_up, ffn1_w_down, norm_mix_g, w_in, rel_bias,
           w_out_att, w_out_ret, w_out, norm_ffn2_g, ffn2_w_gate, ffn2_w_up, ffn2_w_down,
           norm_final_g):
    batch, seq, d = x.shape
    depth = w_in.shape[0]
    assert depth >= 1 and (N_KBLK - 1) * TQ == N_PREV_CHUNKS * CHUNK and seq % TQ == 0
    cos2, sin2 = _rotary_tables(seq)
    xt = x.reshape(batch * seq, d)
    for l in range(depth):
        last = l == depth - 1
        xt, (w_in_b,) = _ffn(xt, norm_ffn1_g[l][None], ffn1_w_gate[l].astype(BF16),
                             ffn1_w_up[l].astype(BF16), ffn1_w_down[l].astype(BF16), cast=(w_in[l],))
        proj, (wg2, wu2, wd2, woa, wor, wo) = _in_proj(
            xt, norm_mix_g[l][None], w_in_b, first_tile_scale=HEAD_DIM ** -0.5 * LOG2E,
            cast=(ffn2_w_gate[l], ffn2_w_up[l], ffn2_w_down[l], w_out_att[l], w_out_ret[l], w_out[l]))
        xt = _mixer_out(proj, xt, cos2, sin2, _rel_row(rel_bias[l]), woa, wor, wo, batch, seq)
        xt, _ = _ffn(xt, norm_ffn2_g[l][None], wg2, wu2, wd2, norm_final_g[None] if last else None)
    return xt.reshape(batch, seq, d)
```

```python
import functools
import math

import numpy as np
import jax
import jax.numpy as jnp
from jax import lax
from jax.experimental import pallas as pl
from jax.experimental.pallas import tpu as pltpu

F32 = jnp.float32
BF16 = jnp.bfloat16

CHUNK = 64
CHUNK_SHIFT = CHUNK.bit_length() - 1
N_PREV_CHUNKS = 8
HEADS = 8
HEAD_DIM = 128
WIDTH = HEADS * HEAD_DIM
MAX_REL_DIST = 128
ROPE_BASE = 10000.0
EPS = 1e-6
NEG = -1e30
LOG2E = math.log2(math.e)

LANES = 128
BF16_SUBLANES = 16

TQ = 256
N_KBLK = 3
FFN_TM = 1024
FFN_TF = 512
OUT_PIECE = 256
MIB = 1024 * 1024
VMEM_LIMIT = 60 * MIB


def _rmsnorm(x, g):
    ms = jnp.mean(x * x, axis=-1, keepdims=True)
    return x * lax.rsqrt(ms + EPS) * g


def _silu(x):
    return x * jax.nn.sigmoid(x)


def _cast_plan(shape, n_i, n_j):
    r, c = shape
    if r % (n_i * n_j * BF16_SUBLANES) == 0:
        return pl.BlockSpec((r // (n_i * n_j), c), lambda i, j: (i * n_j + j, 0)), True
    if r % (n_i * BF16_SUBLANES) == 0 and c % (n_j * LANES) == 0:
        return pl.BlockSpec((r // n_i, c // n_j), lambda i, j: (i, j)), True
    if r % (n_j * BF16_SUBLANES) == 0 and c % (n_i * LANES) == 0:
        return pl.BlockSpec((r // n_j, c // n_i), lambda i, j: (j, i)), True
    if r % (n_i * BF16_SUBLANES) == 0:
        return pl.BlockSpec((r // n_i, c), lambda i, j: (i, 0)), False
    rows = next(k for k in range(BF16_SUBLANES, r + 1, BF16_SUBLANES)
                if r % k == 0 and r // k <= n_i * n_j)
    last = r // rows - 1
    return pl.BlockSpec((rows, c), lambda i, j: (jnp.minimum(i * n_j + j, last), 0)), True


def _cast_side_args(weights, n_i, n_j):
    plans = [_cast_plan(w.shape, n_i, n_j) for w in weights]
    order = ([k for k, (_, every) in enumerate(plans) if every]
             + [k for k, (_, every) in enumerate(plans) if not every])
    specs = [plans[k][0] for k in order]
    n_every = sum(1 for _, every in plans if every)
    return order, specs, n_every


def _unpermute(values, order):
    out = [None] * len(order)
    for pos, k in enumerate(order):
        out[k] = values[pos]
    return out


def _cast_blocks(pairs):
    for src_ref, dst_ref in pairs:
        dst_ref[...] = src_ref[...].astype(BF16)


def _ffn_kernel(*refs, n_f, final_norm, own_cast, has_lead, n_cast, n_cast_every):
    n_in = 6 if final_norm else 5
    n_own = 3 if own_cast else 0
    x_ref, g_ref = refs[:2]
    w_refs = refs[2:5]
    gf_ref = refs[5] if final_norm else None
    lead_ref = refs[n_in + n_cast] if has_lead else None
    n_inputs = n_in + n_cast + (1 if has_lead else 0)
    o_ref = refs[n_inputs]
    own_refs = refs[n_inputs + 1:n_inputs + 1 + n_own]
    casts = list(zip(refs[n_in:n_in + n_cast],
                     refs[n_inputs + 1 + n_own:n_inputs + 1 + n_own + n_cast]))
    cast_every, cast_first = casts[:n_cast_every], casts[n_cast_every:]
    h_ref = refs[n_inputs + 1 + n_own + n_cast]
    i = pl.program_id(0)
    j = pl.program_id(1)

    def partial_down(h):
        if own_cast:
            _cast_blocks(zip(w_refs, own_refs))
        wg_ref, wu_ref, wd_ref = own_refs if own_cast else w_refs
        gate = jnp.dot(h, wg_ref[...], preferred_element_type=F32)
        up = jnp.dot(h, wu_ref[...], preferred_element_type=F32)
        act = (_silu(gate) * up).astype(BF16)
        return jnp.dot(act, wd_ref[...], preferred_element_type=F32)

    def when(cond):
        return pl.when(cond & (i > 0)) if has_lead else pl.when(cond)

    @when(j == 0)
    def _():
        _cast_blocks(cast_every + cast_first)
        h = _rmsnorm(x_ref[...], g_ref[...]).astype(BF16)
        h_ref[...] = h
        o_ref[...] = partial_down(h)

    @when((j > 0) & (j < n_f - 1))
    def _():
        _cast_blocks(cast_every)
        o_ref[...] += partial_down(h_ref[...])

    @when(j == n_f - 1)
    def _():
        _cast_blocks(cast_every)
        y = x_ref[...] + 0.5 * (o_ref[...] + partial_down(h_ref[...]))
        if final_norm:
            y = _rmsnorm(y, gf_ref[...])
        o_ref[...] = y

    if has_lead:
        @pl.when((i == 0) & (j == 0))
        def _():
            _cast_blocks(cast_every + cast_first)
            copy = pltpu.make_async_copy(lead_ref, o_ref, refs[-1])
            copy.start()
            copy.wait()

        @pl.when((i == 0) & (j > 0))
        def _():
            _cast_blocks(cast_every)


def _ffn(x, g, wg, wu, wd, g_final=None, *, tm, tf, cast=(), n_tiles=None, lead=None):
    t, d = x.shape
    f = wg.shape[1]
    n_i = t // tm if n_tiles is None else n_tiles
    n_f = f // tf
    assert n_f >= 2
    final_norm = g_final is not None
    own_cast = wg.dtype == F32
    has_lead = lead is not None
    assert not (own_cast and has_lead)

    def fstep(i, j):
        return jnp.where(i == 0, 0, j) if has_lead else j

    w_specs = [pl.BlockSpec((d, tf), lambda i, j: (0, fstep(i, j))),
               pl.BlockSpec((d, tf), lambda i, j: (0, fstep(i, j))),
               pl.BlockSpec((tf, d), lambda i, j: (fstep(i, j), 0))]
    row_spec = pl.BlockSpec((tm, d), lambda i, j: (i, 0))
    in_specs = [row_spec, pl.BlockSpec((1, d), lambda i, j: (0, 0))] + w_specs
    args = [x, g, wg, wu, wd]
    if final_norm:
        in_specs.append(pl.BlockSpec((1, d), lambda i, j: (0, 0)))
        args.append(g_final)
    order, cast_specs, n_every = _cast_side_args(cast, n_i, n_f)
    in_specs += cast_specs
    args += [cast[k] for k in order]
    scratch = [pltpu.VMEM((tm, d), BF16)]
    if has_lead:
        assert lead.shape == (tm, d)
        in_specs.append(pl.BlockSpec(memory_space=pl.ANY))
        args.append(lead)
        scratch.append(pltpu.SemaphoreType.DMA(()))
    own_specs = w_specs if own_cast else []
    own_shapes = [jax.ShapeDtypeStruct(w.shape, BF16) for w in (wg, wu, wd)] if own_cast else []
    outs = pl.pallas_call(
        functools.partial(_ffn_kernel, n_f=n_f, final_norm=final_norm, own_cast=own_cast,
                          has_lead=has_lead, n_cast=len(cast), n_cast_every=n_every),
        grid=(n_i, n_f),
        in_specs=in_specs,
        out_specs=[row_spec] + own_specs + cast_specs,
        out_shape=[jax.ShapeDtypeStruct((n_i * tm, d), F32)] + own_shapes
        + [jax.ShapeDtypeStruct(cast[k].shape, BF16) for k in order],
        scratch_shapes=scratch,
        compiler_params=pltpu.CompilerParams(
            dimension_semantics=("arbitrary", "arbitrary"),
            vmem_limit_bytes=VMEM_LIMIT),
        name="ffn_final" if final_norm else ("ffn_lead" if own_cast else "ffn"),
    )(*args)
    n_own = len(own_shapes)
    return outs[0], _unpermute(outs[1 + n_own:], order), list(outs[1:1 + n_own])


def _in_proj_kernel(*refs, first_tile_scale, n_cast, n_cast_every):
    x_ref, g_ref, w_ref = refs[:3]
    o_ref = refs[3 + n_cast]
    h_ref = refs[-1]
    casts = list(zip(refs[3:3 + n_cast], refs[4 + n_cast:4 + 2 * n_cast]))
    cast_every, cast_first = casts[:n_cast_every], casts[n_cast_every:]
    j = pl.program_id(1)

    def project(h):
        return jnp.dot(h, w_ref[...], preferred_element_type=F32)

    @pl.when(j == 0)
    def _():
        _cast_blocks(cast_every + cast_first)
        h = _rmsnorm(x_ref[...], g_ref[...]).astype(BF16)
        h_ref[...] = h
        o_ref[...] = (project(h) * first_tile_scale).astype(BF16)

    @pl.when(j > 0)
    def _():
        _cast_blocks(cast_every)
        o_ref[...] = project(h_ref[...]).astype(BF16)


def _in_proj(x, g, w, *, first_tile_scale, cast=(), tm=1024, tn=WIDTH):
    t, d = x.shape
    n = w.shape[1]
    n_i, n_j = t // tm, n // tn
    order, cast_specs, n_every = _cast_side_args(cast, n_i, n_j)
    outs = pl.pallas_call(
        functools.partial(_in_proj_kernel, first_tile_scale=first_tile_scale, n_cast=len(cast),
                          n_cast_every=n_every),
        grid=(n_i, n_j),
        in_specs=[
            pl.BlockSpec((tm, d), lambda i, j: (i, 0)),
            pl.BlockSpec((1, d), lambda i, j: (0, 0)),
            pl.BlockSpec((d, tn), lambda i, j: (0, j)),
        ] + cast_specs,
        out_specs=[pl.BlockSpec((tm, tn), lambda i, j: (i, j))] + cast_specs,
        out_shape=[jax.ShapeDtypeStruct((t, n), BF16)]
        + [jax.ShapeDtypeStruct(cast[k].shape, BF16) for k in order],
        scratch_shapes=[pltpu.VMEM((tm, d), BF16)],
        compiler_params=pltpu.CompilerParams(
            dimension_semantics=("arbitrary", "arbitrary"),
            vmem_limit_bytes=VMEM_LIMIT),
        name="in_proj",
    )(x, g, w, *[cast[k] for k in order])
    return outs[0], _unpermute(outs[1:], order)


_LOG_GAMMA = tuple(math.log(1.0 - 2.0 ** (-5.0 - h)) for h in range(HEADS))


def _dot_nt(a, b):
    return lax.dot_general(a, b, (((1,), (1,)), ((), ())), preferred_element_type=F32)


def _lane_slabs(x):
    return [x[:, k:k + LANES] for k in range(0, x.shape[1], LANES)]


def _dot_tn(a, b):
    return lax.dot_general(a, b, (((0,), (0,)), ((), ())), preferred_element_type=F32)


def _mixer_out_kernel(qa_ref, ka0_ref, ka1_ref, ka2_ref, va0_ref, va1_ref, va2_ref,
                      qr_ref, kr_ref, vr_ref, gr_ref, ga0_ref, ga1_ref, gm0_ref, gm1_ref,
                      cos_ref, sin_ref, relrow_ref, xprev_ref, woa_ref, wor_ref, wo_ref,
                      o_ref,
                      bias_ref, decay_ref, state_ref,
                      att_ref, ret_ref, attp_ref, retp_ref, merged_ref, *, n_tiles, tiles_per_seq):
    s = pl.program_id(0)
    t = jnp.minimum(s, n_tiles - 1)
    i = lax.rem(t, tiles_per_seq)
    ka_refs = (ka0_ref, ka1_ref, ka2_ref)
    va_refs = (va0_ref, va1_ref, va2_ref)

    @pl.when(s == 0)
    def _():
        qc = lax.broadcasted_iota(jnp.int32, (TQ, N_KBLK * TQ), 0) >> CHUNK_SHIFT
        kc = lax.broadcasted_iota(jnp.int32, (TQ, N_KBLK * TQ), 1) >> CHUNK_SHIFT
        in_band = (kc >= qc) & (kc <= qc + N_PREV_CHUNKS)
        for h in range(HEADS):
            rows = jnp.broadcast_to(relrow_ref[h:h + 1, :], (TQ, (N_KBLK + 1) * TQ))
            toeplitz = pltpu.roll(rows, 0, axis=1, stride=1, stride_axis=0)
            bias_ref[h] = jnp.where(in_band, toeplitz[:, TQ:] * LOG2E, NEG)
        bias_ref[HEADS] = jnp.full((TQ, N_KBLK * TQ), NEG, F32)

        n = lax.broadcasted_iota(jnp.int32, (TQ, TQ), 0)
        m = lax.broadcasted_iota(jnp.int32, (TQ, TQ), 1)
        dist = jnp.abs(n - m).astype(F32)
        visible = (m >> CHUNK_SHIFT) <= (n >> CHUNK_SHIFT)
        for h in range(HEADS):
            decay_ref[h] = jnp.where(visible, jnp.exp(_LOG_GAMMA[h] * dist), 0.0)
        att_ref[...] = jnp.zeros_like(att_ref)
        ret_ref[...] = jnp.zeros_like(ret_ref)

    @pl.when(i == 0)
    def _():
        state_ref[...] = jnp.zeros_like(state_ref)

    attp_ref[...] = att_ref[...]
    retp_ref[...] = ret_ref[...]

    ret_scale = HEAD_DIM ** -0.5
    cos = cos_ref[...]
    sin = sin_ref[...]
    row = lax.broadcasted_iota(jnp.int32, (TQ, HEAD_DIM), 0).astype(F32)

    def rot(x):
        return x * cos + pltpu.roll(x, HEAD_DIM // 2, axis=1) * sin

    def scores(h):
        cols = slice(h * HEAD_DIM, (h + 1) * HEAD_DIM)
        q = qa_ref[:, cols]
        sc = []
        for j in range(N_KBLK):
            plane = h if j == N_KBLK - 1 else jnp.where(i + j >= N_KBLK - 1, h, HEADS)
            sc.append(_dot_nt(q, ka_refs[j][:, cols]) + bias_ref[plane, :, j * TQ:(j + 1) * TQ])
        return sc

    def attend(h, sc):
        cols = slice(h * HEAD_DIM, (h + 1) * HEAD_DIM)
        mx = jnp.maximum(jnp.maximum(jnp.max(sc[0], axis=-1, keepdims=True),
                                     jnp.max(sc[1], axis=-1, keepdims=True)),
                         jnp.max(sc[2], axis=-1, keepdims=True))
        acc = None
        den = None
        for j in range(N_KBLK):
            p = jnp.exp2(sc[j] - mx)
            dj = jnp.sum(p, axis=-1, keepdims=True)
            oj = jnp.dot(p.astype(BF16), va_refs[j][:, cols], preferred_element_type=F32)
            acc = oj if acc is None else acc + oj
            den = dj if den is None else den + dj
        att_ref[:, cols] = (acc / den).astype(BF16)

    def retain(h, fill):
        cols = slice(h * HEAD_DIM, (h + 1) * HEAD_DIM)
        qr = (rot(qr_ref[:, cols].astype(F32)) * ret_scale).astype(BF16)
        kr_f = rot(kr_ref[:, cols].astype(F32))
        kr = kr_f.astype(BF16)
        v = vr_ref[:, cols]
        sr = _dot_nt(qr, kr)
        st = state_ref[h]
        lg = _LOG_GAMMA[h]
        cross = jnp.dot(qr, st.astype(BF16), preferred_element_type=F32)
        fill()
        inner = jnp.dot((sr * decay_ref[h]).astype(BF16), v, preferred_element_type=F32)
        kd = (kr_f * jnp.exp(lg * (TQ - 1.0 - row))).astype(BF16)
        state_ref[h] = st * math.exp(lg * TQ) + _dot_tn(kd, v)
        out = inner + cross * jnp.exp(lg * (row + 1.0))
        out = out * lax.rsqrt(jnp.mean(out * out, axis=-1, keepdims=True) + EPS)
        ret_ref[:, cols] = (_silu(gr_ref[:, cols].astype(F32)) * out).astype(BF16)

    gate_refs = ((ga0_ref, gm0_ref), (ga1_ref, gm1_ref))

    def merge_piece(k):
        cols = slice(k * OUT_PIECE, (k + 1) * OUT_PIECE)
        ga_ref, gm_ref = gate_refs[k * OUT_PIECE // WIDTH]
        gcols = slice(k * OUT_PIECE % WIDTH, k * OUT_PIECE % WIDTH + OUT_PIECE)
        branch_a = jnp.dot(attp_ref[...], woa_ref[:, cols], preferred_element_type=F32)
        branch_r = jnp.dot(retp_ref[...], wor_ref[:, cols], preferred_element_type=F32)
        merged = (jax.nn.sigmoid(ga_ref[:, gcols].astype(F32)) * branch_a
                  + jax.nn.sigmoid(gm_ref[:, gcols].astype(F32)) * branch_r)
        merged_ref[:, cols] = merged.astype(BF16)

    def out_piece(k):
        cols = slice(k * OUT_PIECE, (k + 1) * OUT_PIECE)
        o_ref[:, cols] = xprev_ref[:, cols] + jnp.dot(merged_ref[...], wo_ref[:, cols],
                                                      preferred_element_type=F32)

    n_pieces = merged_ref.shape[1] // OUT_PIECE
    pieces = ([functools.partial(merge_piece, k) for k in range(n_pieces)]
              + [functools.partial(out_piece, k) for k in range(n_pieces)])
    assert len(pieces) == 2 * HEADS
    sc = scores(0)
    for h in range(HEADS):
        sc_next = scores(h + 1) if h + 1 < HEADS else None
        pieces[2 * h]()
        attend(h, sc)
        retain(h, pieces[2 * h + 1])
        sc = sc_next


def _mixer_out(proj, x, cos2, sin2, rel_row, woa, wor, wo, batch, seq):
    t, d = x.shape
    tiles_per_seq = seq // TQ
    n_tiles = batch * tiles_per_seq
    proj3 = proj.reshape(batch, seq, proj.shape[-1])

    def split(tt):
        return tt // tiles_per_seq, lax.rem(tt, tiles_per_seq)

    def cur(s):
        return split(jnp.minimum(s, n_tiles - 1))

    def tile(col):
        return pl.BlockSpec((None, TQ, WIDTH), lambda s: (*cur(s), col))

    def prev_tile(col):
        return pl.BlockSpec((None, TQ, WIDTH), lambda s: (*split(jnp.maximum(s - 1, 0)), col))

    def hist(col, j):
        def index(s):
            b, i = cur(s)
            return b, jnp.maximum(i - (N_KBLK - 1) + j, 0), col
        return pl.BlockSpec((None, TQ, WIDTH), index)

    def table():
        return pl.BlockSpec((TQ, HEAD_DIM), lambda s: (cur(s)[1], 0))

    def resident(shape):
        return pl.BlockSpec(shape, lambda s: (0, 0), pipeline_mode=pl.Buffered(1))

    prev = pl.BlockSpec((TQ, d), lambda s: (jnp.maximum(s - 1, 0), 0))
    gate0 = 7
    in_specs = ([tile(0)] + [hist(1, j) for j in range(N_KBLK)] + [hist(2, j) for j in range(N_KBLK)]
                + [tile(3), tile(4), tile(5), tile(6)]
                + [prev_tile(gate0 + c) for c in range(4)]
                + [table(), table(), resident(rel_row.shape), prev,
                   resident(woa.shape), resident(wor.shape), resident(wo.shape)])
    return pl.pallas_call(
        functools.partial(_mixer_out_kernel, n_tiles=n_tiles, tiles_per_seq=tiles_per_seq),
        grid=(n_tiles + 1,),
        in_specs=in_specs,
        out_specs=prev,
        out_shape=jax.ShapeDtypeStruct((t, d), F32),
        scratch_shapes=[pltpu.VMEM((HEADS + 1, TQ, N_KBLK * TQ), F32),
                        pltpu.VMEM((HEADS, TQ, TQ), F32),
                        pltpu.VMEM((HEADS, HEAD_DIM, HEAD_DIM), F32),
                        pltpu.VMEM((TQ, WIDTH), BF16),
                        pltpu.VMEM((TQ, WIDTH), BF16),
                        pltpu.VMEM((TQ, WIDTH), BF16),
                        pltpu.VMEM((TQ, WIDTH), BF16),
                        pltpu.VMEM((TQ, d), BF16)],
        compiler_params=pltpu.CompilerParams(
            dimension_semantics=("arbitrary",),
            vmem_limit_bytes=VMEM_LIMIT),
        name="mixer_out",
    )(*([proj3] * 15), cos2, sin2, rel_row, x, woa, wor, wo)


def _rotary_tables(seq):
    inv = 1.0 / (ROPE_BASE ** (jnp.arange(0, HEAD_DIM, 2, dtype=F32) / HEAD_DIM))
    ang = jnp.arange(seq, dtype=F32)[:, None] * inv[None, :]
    cos = jnp.cos(ang)
    sin = jnp.sin(ang)
    return jnp.concatenate([cos, cos], axis=-1), jnp.concatenate([-sin, sin], axis=-1)


def _rel_row(rel_bias):
    j = np.arange((N_KBLK + 1) * TQ)
    rel_idx = np.clip(N_KBLK * TQ - j, -MAX_REL_DIST, MAX_REL_DIST) + MAX_REL_DIST
    return rel_bias.astype(F32)[:, rel_idx]


def kernel(x, norm_ffn1_g, ffn1_w_gate, ffn1_w_up, ffn1_w_down, norm_mix_g, w_in, rel_bias,
           w_out_att, w_out_ret, w_out, norm_ffn2_g, ffn2_w_gate, ffn2_w_up, ffn2_w_down,
           norm_final_g):
    batch, seq, d = x.shape
    depth = w_in.shape[0]
    assert depth >= 1 and (N_KBLK - 1) * TQ == N_PREV_CHUNKS * CHUNK and seq % TQ == 0
    cos2, sin2 = _rotary_tables(seq)
    xt = x.reshape(batch * seq, d)
    for l in range(depth):
        last = l == depth - 1
        g1 = norm_ffn1_g[l][None]
        y_lead, _, (wg1, wu1, wd1) = _ffn(xt, g1, ffn1_w_gate[l], ffn1_w_up[l], ffn1_w_down[l],
                                          n_tiles=1, tm=FFN_TM, tf=FFN_TF // 2)
        xt, (w_in_b,), _ = _ffn(xt, g1, wg1, wu1, wd1, cast=(w_in[l],), lead=y_lead,
                                tm=FFN_TM, tf=FFN_TF)
        proj, (wg2, wu2, wd2, woa, wor, wo) = _in_proj(
            xt, norm_mix_g[l][None], w_in_b, first_tile_scale=HEAD_DIM ** -0.5 * LOG2E,
            cast=(ffn2_w_gate[l], ffn2_w_up[l], ffn2_w_down[l], w_out_att[l], w_out_ret[l], w_out[l]))
        xt = _mixer_out(proj, xt, cos2, sin2, _rel_row(rel_bias[l]), woa, wor, wo, batch, seq)
        xt, _, _ = _ffn(xt, norm_ffn2_g[l][None], wg2, wu2, wd2, norm_final_g[None] if last else None,
                        tm=FFN_TM, tf=FFN_TF)
    return xt.reshape(batch, seq, d)
```

```python
import functools
import math

import numpy as np
import jax
import jax.numpy as jnp
from jax import lax
from jax.experimental import pallas as pl
from jax.experimental.pallas import tpu as pltpu

F32 = jnp.float32
BF16 = jnp.bfloat16

CHUNK = 64
CHUNK_SHIFT = CHUNK.bit_length() - 1
N_PREV_CHUNKS = 8
HEADS = 8
HEAD_DIM = 128
WIDTH = HEADS * HEAD_DIM
MAX_REL_DIST = 128
ROPE_BASE = 10000.0
EPS = 1e-6
NEG = -1e30
LOG2E = math.log2(math.e)

LANES = 128
BF16_SUBLANES = 16

TQ = 256
N_KBLK = 3
FFN_TM = 1024
FFN_TF = 512
OUT_PIECE = 256
MIB = 1024 * 1024
VMEM_LIMIT = 60 * MIB


def _rmsnorm(x, g):
    ms = jnp.mean(x * x, axis=-1, keepdims=True)
    return x * lax.rsqrt(ms + EPS) * g


def _silu(x):
    return x * jax.nn.sigmoid(x)


def _cast_plan(shape, n_i, n_j):
    r, c = shape
    if r % (n_i * n_j * BF16_SUBLANES) == 0:
        return pl.BlockSpec((r // (n_i * n_j), c), lambda i, j: (i * n_j + j, 0)), True
    if r % (n_i * BF16_SUBLANES) == 0 and c % (n_j * LANES) == 0:
        return pl.BlockSpec((r // n_i, c // n_j), lambda i, j: (i, j)), True
    if r % (n_j * BF16_SUBLANES) == 0 and c % (n_i * LANES) == 0:
        return pl.BlockSpec((r // n_j, c // n_i), lambda i, j: (j, i)), True
    if r % (n_i * BF16_SUBLANES) == 0:
        return pl.BlockSpec((r // n_i, c), lambda i, j: (i, 0)), False
    rows = next(k for k in range(BF16_SUBLANES, r + 1, BF16_SUBLANES)
                if r % k == 0 and r // k <= n_i * n_j)
    last = r // rows - 1
    return pl.BlockSpec((rows, c), lambda i, j: (jnp.minimum(i * n_j + j, last), 0)), True


def _cast_side_args(weights, n_i, n_j):
    plans = [_cast_plan(w.shape, n_i, n_j) for w in weights]
    order = ([k for k, (_, every) in enumerate(plans) if every]
             + [k for k, (_, every) in enumerate(plans) if not every])
    specs = [plans[k][0] for k in order]
    n_every = sum(1 for _, every in plans if every)
    return order, specs, n_every


def _unpermute(values, order):
    out = [None] * len(order)
    for pos, k in enumerate(order):
        out[k] = values[pos]
    return out


def _cast_blocks(pairs):
    for src_ref, dst_ref in pairs:
        dst_ref[...] = src_ref[...].astype(BF16)


def _ffn_kernel(*refs, n_f, final_norm, own_cast, has_lead, n_cast, n_cast_every):
    n_in = 6 if final_norm else 5
    n_own = 3 if own_cast else 0
    x_ref, g_ref = refs[:2]
    w_refs = refs[2:5]
    gf_ref = refs[5] if final_norm else None
    lead_ref = refs[n_in + n_cast] if has_lead else None
    n_inputs = n_in + n_cast + (1 if has_lead else 0)
    o_ref = refs[n_inputs]
    own_refs = refs[n_inputs + 1:n_inputs + 1 + n_own]
    casts = list(zip(refs[n_in:n_in + n_cast],
                     refs[n_inputs + 1 + n_own:n_inputs + 1 + n_own + n_cast]))
    cast_every, cast_first = casts[:n_cast_every], casts[n_cast_every:]
    h_ref = refs[n_inputs + 1 + n_own + n_cast]
    i = pl.program_id(0)
    j = pl.program_id(1)

    def partial_down(h):
        if own_cast:
            _cast_blocks(zip(w_refs, own_refs))
        wg_ref, wu_ref, wd_ref = own_refs if own_cast else w_refs
        gate = jnp.dot(h, wg_ref[...], preferred_element_type=F32)
        up = jnp.dot(h, wu_ref[...], preferred_element_type=F32)
        act = (_silu(gate) * up).astype(BF16)
        return jnp.dot(act, wd_ref[...], preferred_element_type=F32)

    def when(cond):
        return pl.when(cond & (i > 0)) if has_lead else pl.when(cond)

    @when(j == 0)
    def _():
        _cast_blocks(cast_every + cast_first)
        h = _rmsnorm(x_ref[...], g_ref[...]).astype(BF16)
        h_ref[...] = h
        o_ref[...] = partial_down(h)

    @when((j > 0) & (j < n_f - 1))
    def _():
        _cast_blocks(cast_every)
        o_ref[...] += partial_down(h_ref[...])

    @when(j == n_f - 1)
    def _():
        _cast_blocks(cast_every)
        y = x_ref[...] + 0.5 * (o_ref[...] + partial_down(h_ref[...]))
        if final_norm:
            y = _rmsnorm(y, gf_ref[...])
        o_ref[...] = y

    if has_lead:
        @pl.when((i == 0) & (j == 0))
        def _():
            _cast_blocks(cast_every + cast_first)
            copy = pltpu.make_async_copy(lead_ref, o_ref, refs[-1])
            copy.start()
            copy.wait()

        @pl.when((i == 0) & (j > 0))
        def _():
            _cast_blocks(cast_every)


def _ffn(x, g, wg, wu, wd, g_final=None, *, tm, tf, cast=(), n_tiles=None, lead=None):
    t, d = x.shape
    f = wg.shape[1]
    n_i = t // tm if n_tiles is None else n_tiles
    n_f = f // tf
    assert n_f >= 2
    final_norm = g_final is not None
    own_cast = wg.dtype == F32
    has_lead = lead is not None
    assert not (own_cast and has_lead)

    def fstep(i, j):
        return jnp.where(i == 0, 0, j) if has_lead else j

    w_specs = [pl.BlockSpec((d, tf), lambda i, j: (0, fstep(i, j))),
               pl.BlockSpec((d, tf), lambda i, j: (0, fstep(i, j))),
               pl.BlockSpec((tf, d), lambda i, j: (fstep(i, j), 0))]
    row_spec = pl.BlockSpec((tm, d), lambda i, j: (i, 0))
    in_specs = [row_spec, pl.BlockSpec((1, d), lambda i, j: (0, 0))] + w_specs
    args = [x, g, wg, wu, wd]
    if final_norm:
        in_specs.append(pl.BlockSpec((1, d), lambda i, j: (0, 0)))
        args.append(g_final)
    order, cast_specs, n_every = _cast_side_args(cast, n_i, n_f)
    in_specs += cast_specs
    args += [cast[k] for k in order]
    scratch = [pltpu.VMEM((tm, d), BF16)]
    if has_lead:
        assert lead.shape == (tm, d)
        in_specs.append(pl.BlockSpec(memory_space=pl.ANY))
        args.append(lead)
        scratch.append(pltpu.SemaphoreType.DMA(()))
    own_specs = w_specs if own_cast else []
    own_shapes = [jax.ShapeDtypeStruct(w.shape, BF16) for w in (wg, wu, wd)] if own_cast else []
    outs = pl.pallas_call(
        functools.partial(_ffn_kernel, n_f=n_f, final_norm=final_norm, own_cast=own_cast,
                          has_lead=has_lead, n_cast=len(cast), n_cast_every=n_every),
        grid=(n_i, n_f),
        in_specs=in_specs,
        out_specs=[row_spec] + own_specs + cast_specs,
        out_shape=[jax.ShapeDtypeStruct((n_i * tm, d), F32)] + own_shapes
        + [jax.ShapeDtypeStruct(cast[k].shape, BF16) for k in order],
        scratch_shapes=scratch,
        compiler_params=pltpu.CompilerParams(
            dimension_semantics=("arbitrary", "arbitrary"),
            vmem_limit_bytes=VMEM_LIMIT),
        name="ffn_final" if final_norm else ("ffn_lead" if own_cast else "ffn"),
    )(*args)
    n_own = len(own_shapes)
    return outs[0], _unpermute(outs[1 + n_own:], order), list(outs[1:1 + n_own])


def _in_proj_kernel(*refs, first_tile_scale, rot_tiles, n_cast, n_cast_every):
    x_ref, g_ref, w_ref, cos_ref, sin_ref = refs[:5]
    o_ref = refs[5 + n_cast]
    h_ref = refs[-1]
    casts = list(zip(refs[5:5 + n_cast], refs[6 + n_cast:6 + 2 * n_cast]))
    cast_every, cast_first = casts[:n_cast_every], casts[n_cast_every:]
    j = pl.program_id(1)

    def project(h):
        return jnp.dot(h, w_ref[...], preferred_element_type=F32)

    @pl.when(j == 0)
    def _():
        _cast_blocks(cast_every + cast_first)
        h = _rmsnorm(x_ref[...], g_ref[...]).astype(BF16)
        h_ref[...] = h
        o_ref[...] = (project(h) * first_tile_scale).astype(BF16)

    plain = j > 0
    for tile, scale in rot_tiles:
        plain = plain & (j != tile)

        @pl.when(j == tile)
        def _(scale=scale):
            _cast_blocks(cast_every)
            acc = project(h_ref[...])
            cos = cos_ref[...]
            sin = sin_ref[...]
            for hd in range(o_ref.shape[1] // HEAD_DIM):
                cols = slice(hd * HEAD_DIM, (hd + 1) * HEAD_DIM)
                xh = acc[:, cols]
                rot = xh * cos + pltpu.roll(xh, HEAD_DIM // 2, axis=1) * sin
                o_ref[:, cols] = (rot * scale).astype(BF16)

    @pl.when(plain)
    def _():
        _cast_blocks(cast_every)
        o_ref[...] = project(h_ref[...]).astype(BF16)


def _in_proj(x, g, w, cos2, sin2, *, first_tile_scale, rot_tiles, cast=(), tm=1024, tn=WIDTH):
    t, d = x.shape
    n = w.shape[1]
    n_i, n_j = t // tm, n // tn
    seq_tiles = cos2.shape[0] // tm
    order, cast_specs, n_every = _cast_side_args(cast, n_i, n_j)
    table = pl.BlockSpec((tm, HEAD_DIM), lambda i, j: (lax.rem(i, seq_tiles), 0))
    outs = pl.pallas_call(
        functools.partial(_in_proj_kernel, first_tile_scale=first_tile_scale, rot_tiles=rot_tiles,
                          n_cast=len(cast), n_cast_every=n_every),
        grid=(n_i, n_j),
        in_specs=[
            pl.BlockSpec((tm, d), lambda i, j: (i, 0)),
            pl.BlockSpec((1, d), lambda i, j: (0, 0)),
            pl.BlockSpec((d, tn), lambda i, j: (0, j)),
            table, table,
        ] + cast_specs,
        out_specs=[pl.BlockSpec((tm, tn), lambda i, j: (i, j))] + cast_specs,
        out_shape=[jax.ShapeDtypeStruct((t, n), BF16)]
        + [jax.ShapeDtypeStruct(cast[k].shape, BF16) for k in order],
        scratch_shapes=[pltpu.VMEM((tm, d), BF16)],
        compiler_params=pltpu.CompilerParams(
            dimension_semantics=("arbitrary", "arbitrary"),
            vmem_limit_bytes=VMEM_LIMIT),
        name="in_proj",
    )(x, g, w, cos2, sin2, *[cast[k] for k in order])
    return outs[0], _unpermute(outs[1:], order)


_LOG_GAMMA = tuple(math.log(1.0 - 2.0 ** (-5.0 - h)) for h in range(HEADS))


def _dot_nt(a, b):
    return lax.dot_general(a, b, (((1,), (1,)), ((), ())), preferred_element_type=F32)


def _lane_slabs(x):
    return [x[:, k:k + LANES] for k in range(0, x.shape[1], LANES)]


def _dot_tn(a, b):
    return lax.dot_general(a, b, (((0,), (0,)), ((), ())), preferred_element_type=F32)


def _mixer_out_kernel(qa_ref, ka0_ref, ka1_ref, ka2_ref, va0_ref, va1_ref, va2_ref,
                      qr_ref, kr_ref, vr_ref, gr_ref, ga0_ref, ga1_ref, gm0_ref, gm1_ref,
                      relrow_ref, xprev_ref, woa_ref, wor_ref, wo_ref,
                      o_ref,
                      bias_ref, decay_ref, state_ref,
                      att_ref, ret_ref, attp_ref, retp_ref, merged_ref, *, n_tiles, tiles_per_seq):
    s = pl.program_id(0)
    t = jnp.minimum(s, n_tiles - 1)
    i = lax.rem(t, tiles_per_seq)
    ka_refs = (ka0_ref, ka1_ref, ka2_ref)
    va_refs = (va0_ref, va1_ref, va2_ref)

    @pl.when(s == 0)
    def _():
        qc = lax.broadcasted_iota(jnp.int32, (TQ, N_KBLK * TQ), 0) >> CHUNK_SHIFT
        kc = lax.broadcasted_iota(jnp.int32, (TQ, N_KBLK * TQ), 1) >> CHUNK_SHIFT
        in_band = (kc >= qc) & (kc <= qc + N_PREV_CHUNKS)
        for h in range(HEADS):
            rows = jnp.broadcast_to(relrow_ref[h:h + 1, :], (TQ, (N_KBLK + 1) * TQ))
            toeplitz = pltpu.roll(rows, 0, axis=1, stride=1, stride_axis=0)
            bias_ref[h] = jnp.where(in_band, toeplitz[:, TQ:] * LOG2E, NEG)
        bias_ref[HEADS] = jnp.full((TQ, N_KBLK * TQ), NEG, F32)

        n = lax.broadcasted_iota(jnp.int32, (TQ, TQ), 0)
        m = lax.broadcasted_iota(jnp.int32, (TQ, TQ), 1)
        dist = jnp.abs(n - m).astype(F32)
        visible = (m >> CHUNK_SHIFT) <= (n >> CHUNK_SHIFT)
        for h in range(HEADS):
            decay_ref[h] = jnp.where(visible, jnp.exp(_LOG_GAMMA[h] * dist), 0.0)
        att_ref[...] = jnp.zeros_like(att_ref)
        ret_ref[...] = jnp.zeros_like(ret_ref)

    @pl.when(i == 0)
    def _():
        state_ref[...] = jnp.zeros_like(state_ref)

    attp_ref[...] = att_ref[...]
    retp_ref[...] = ret_ref[...]

    row = lax.broadcasted_iota(jnp.int32, (TQ, HEAD_DIM), 0).astype(F32)

    def scores(h):
        cols = slice(h * HEAD_DIM, (h + 1) * HEAD_DIM)
        q = qa_ref[:, cols]
        sc = []
        for j in range(N_KBLK):
            plane = h if j == N_KBLK - 1 else jnp.where(i + j >= N_KBLK - 1, h, HEADS)
            sc.append(_dot_nt(q, ka_refs[j][:, cols]) + bias_ref[plane, :, j * TQ:(j + 1) * TQ])
        return sc

    def attend(h, sc):
        cols = slice(h * HEAD_DIM, (h + 1) * HEAD_DIM)
        mx = jnp.maximum(jnp.maximum(jnp.max(sc[0], axis=-1, keepdims=True),
                                     jnp.max(sc[1], axis=-1, keepdims=True)),
                         jnp.max(sc[2], axis=-1, keepdims=True))
        acc = None
        den = None
        for j in range(N_KBLK):
            p = jnp.exp2(sc[j] - mx)
            dj = jnp.sum(p, axis=-1, keepdims=True)
            oj = jnp.dot(p.astype(BF16), va_refs[j][:, cols], preferred_element_type=F32)
            acc = oj if acc is None else acc + oj
            den = dj if den is None else den + dj
        att_ref[:, cols] = (acc / den).astype(BF16)

    def retain(h, fill):
        cols = slice(h * HEAD_DIM, (h + 1) * HEAD_DIM)
        qr = qr_ref[:, cols]
        kr = kr_ref[:, cols]
        kr_f = kr.astype(F32)
        v = vr_ref[:, cols]
        sr = _dot_nt(qr, kr)
        st = state_ref[h]
        lg = _LOG_GAMMA[h]
        cross = jnp.dot(qr, st.astype(BF16), preferred_element_type=F32)
        fill()
        inner = jnp.dot((sr * decay_ref[h]).astype(BF16), v, preferred_element_type=F32)
        kd = (kr_f * jnp.exp(lg * (TQ - 1.0 - row))).astype(BF16)
        state_ref[h] = st * math.exp(lg * TQ) + _dot_tn(kd, v)
        out = inner + cross * jnp.exp(lg * (row + 1.0))
        out = out * lax.rsqrt(jnp.mean(out * out, axis=-1, keepdims=True) + EPS)
        ret_ref[:, cols] = (_silu(gr_ref[:, cols].astype(F32)) * out).astype(BF16)

    gate_refs = ((ga0_ref, gm0_ref), (ga1_ref, gm1_ref))

    def merge_piece(k):
        cols = slice(k * OUT_PIECE, (k + 1) * OUT_PIECE)
        ga_ref, gm_ref = gate_refs[k * OUT_PIECE // WIDTH]
        gcols = slice(k * OUT_PIECE % WIDTH, k * OUT_PIECE % WIDTH + OUT_PIECE)
        branch_a = jnp.dot(attp_ref[...], woa_ref[:, cols], preferred_element_type=F32)
        branch_r = jnp.dot(retp_ref[...], wor_ref[:, cols], preferred_element_type=F32)
        merged = (jax.nn.sigmoid(ga_ref[:, gcols].astype(F32)) * branch_a
                  + jax.nn.sigmoid(gm_ref[:, gcols].astype(F32)) * branch_r)
        merged_ref[:, cols] = merged.astype(BF16)

    def out_piece(k):
        cols = slice(k * OUT_PIECE, (k + 1) * OUT_PIECE)
        o_ref[:, cols] = xprev_ref[:, cols] + jnp.dot(merged_ref[...], wo_ref[:, cols],
                                                      preferred_element_type=F32)

    n_pieces = merged_ref.shape[1] // OUT_PIECE
    pieces = ([functools.partial(merge_piece, k) for k in range(n_pieces)]
              + [functools.partial(out_piece, k) for k in range(n_pieces)])
    assert len(pieces) == 2 * HEADS
    sc = scores(0)
    for h in range(HEADS):
        sc_next = scores(h + 1) if h + 1 < HEADS else None
        pieces[2 * h]()
        attend(h, sc)
        retain(h, pieces[2 * h + 1])
        sc = sc_next


def _mixer_out(proj, x, rel_row, woa, wor, wo, batch, seq):
    t, d = x.shape
    tiles_per_seq = seq // TQ
    n_tiles = batch * tiles_per_seq
    proj3 = proj.reshape(batch, seq, proj.shape[-1])

    def split(tt):
        return tt // tiles_per_seq, lax.rem(tt, tiles_per_seq)

    def cur(s):
        return split(jnp.minimum(s, n_tiles - 1))

    def tile(col):
        return pl.BlockSpec((None, TQ, WIDTH), lambda s: (*cur(s), col))

    def prev_tile(col):
        return pl.BlockSpec((None, TQ, WIDTH), lambda s: (*split(jnp.maximum(s - 1, 0)), col))

    def hist(col, j):
        def index(s):
            b, i = cur(s)
            return b, jnp.maximum(i - (N_KBLK - 1) + j, 0), col
        return pl.BlockSpec((None, TQ, WIDTH), index)

    def resident(shape):
        return pl.BlockSpec(shape, lambda s: (0, 0), pipeline_mode=pl.Buffered(1))

    prev = pl.BlockSpec((TQ, d), lambda s: (jnp.maximum(s - 1, 0), 0))
    gate0 = 7
    in_specs = ([tile(0)] + [hist(1, j) for j in range(N_KBLK)] + [hist(2, j) for j in range(N_KBLK)]
                + [tile(3), tile(4), tile(5), tile(6)]
                + [prev_tile(gate0 + c) for c in range(4)]
                + [resident(rel_row.shape), prev,
                   resident(woa.shape), resident(wor.shape), resident(wo.shape)])
    return pl.pallas_call(
        functools.partial(_mixer_out_kernel, n_tiles=n_tiles, tiles_per_seq=tiles_per_seq),
        grid=(n_tiles + 1,),
        in_specs=in_specs,
        out_specs=prev,
        out_shape=jax.ShapeDtypeStruct((t, d), F32),
        scratch_shapes=[pltpu.VMEM((HEADS + 1, TQ, N_KBLK * TQ), F32),
                        pltpu.VMEM((HEADS, TQ, TQ), F32),
                        pltpu.VMEM((HEADS, HEAD_DIM, HEAD_DIM), F32),
                        pltpu.VMEM((TQ, WIDTH), BF16),
                        pltpu.VMEM((TQ, WIDTH), BF16),
                        pltpu.VMEM((TQ, WIDTH), BF16),
                        pltpu.VMEM((TQ, WIDTH), BF16),
                        pltpu.VMEM((TQ, d), BF16)],
        compiler_params=pltpu.CompilerParams(
            dimension_semantics=("arbitrary",),
            vmem_limit_bytes=VMEM_LIMIT),
        name="mixer_out",
    )(*([proj3] * 15), rel_row, x, woa, wor, wo)


def _rotary_tables(seq):
    inv = 1.0 / (ROPE_BASE ** (np.arange(0, HEAD_DIM, 2, dtype=np.float64) / HEAD_DIM))
    ang = np.arange(seq, dtype=np.float64)[:, None] * inv[None, :]
    cos, sin = np.cos(ang), np.sin(ang)
    return (np.concatenate([cos, cos], axis=-1).astype(np.float32),
            np.concatenate([-sin, sin], axis=-1).astype(np.float32))


def _rel_row(rel_bias):
    j = np.arange((N_KBLK + 1) * TQ)
    rel_idx = np.clip(N_KBLK * TQ - j, -MAX_REL_DIST, MAX_REL_DIST) + MAX_REL_DIST
    return rel_bias.astype(F32)[:, rel_idx]


def kernel(x, norm_ffn1_g, ffn1_w_gate, ffn1_w_up, ffn1_w_down, norm_mix_g, w_in, rel_bias,
           w_out_att, w_out_ret, w_out, norm_ffn2_g, ffn2_w_gate, ffn2_w_up, ffn2_w_down,
           norm_final_g):
    batch, seq, d = x.shape
    depth = w_in.shape[0]
    assert depth >= 1 and (N_KBLK - 1) * TQ == N_PREV_CHUNKS * CHUNK and seq % TQ == 0
    cos2, sin2 = _rotary_tables(seq)
    xt = x.reshape(batch * seq, d)
    for l in range(depth):
        last = l == depth - 1
        g1 = norm_ffn1_g[l][None]
        y_lead, _, (wg1, wu1, wd1) = _ffn(xt, g1, ffn1_w_gate[l], ffn1_w_up[l], ffn1_w_down[l],
                                          n_tiles=1, tm=FFN_TM, tf=FFN_TF // 2)
        xt, (w_in_b, wg2, wu2, wd2), _ = _ffn(
            xt, g1, wg1, wu1, wd1, lead=y_lead, tm=FFN_TM, tf=FFN_TF,
            cast=(w_in[l], ffn2_w_gate[l], ffn2_w_up[l], ffn2_w_down[l]))
        proj, (woa, wor, wo) = _in_proj(
            xt, norm_mix_g[l][None], w_in_b, cos2, sin2, first_tile_scale=HEAD_DIM ** -0.5 * LOG2E,
            rot_tiles=((3, HEAD_DIM ** -0.5), (4, 1.0)),
            cast=(w_out_att[l], w_out_ret[l], w_out[l]))
        xt = _mixer_out(proj, xt, _rel_row(rel_bias[l]), woa, wor, wo, batch, seq)
        xt, _, _ = _ffn(xt, norm_ffn2_g[l][None], wg2, wu2, wd2, norm_final_g[None] if last else None,
                        tm=FFN_TM, tf=FFN_TF)
    return xt.reshape(batch, seq, d)
```

```python
import functools
import math

import numpy as np
import jax
import jax.numpy as jnp
from jax import lax
from jax.experimental import pallas as pl
from jax.experimental.pallas import tpu as pltpu

F32 = jnp.float32
BF16 = jnp.bfloat16

CHUNK = 64
CHUNK_SHIFT = CHUNK.bit_length() - 1
N_PREV_CHUNKS = 8
HEADS = 8
HEAD_DIM = 128
WIDTH = HEADS * HEAD_DIM
MAX_REL_DIST = 128
ROPE_BASE = 10000.0
EPS = 1e-6
NEG = -1e30
LOG2E = math.log2(math.e)

LANES = 128
BF16_SUBLANES = 16

TQ = 256
N_KBLK = 3
FFN_TM = 1024
FFN_TF = 512
OUT_PIECE = 256
MIB = 1024 * 1024
VMEM_LIMIT = 60 * MIB


def _rmsnorm(x, g):
    ms = jnp.mean(x * x, axis=-1, keepdims=True)
    return x * lax.rsqrt(ms + EPS) * g


def _silu(x):
    return x * jax.nn.sigmoid(x)


def _cast_plan(shape, n_i, n_j, blocked):
    r, c = shape
    same = lambda spec, every: (spec, spec, shape, every)
    if blocked:
        assert r % (n_i * BF16_SUBLANES) == 0 and c % (n_j * LANES) == 0, shape
        return (pl.BlockSpec((r // n_i, c // n_j), lambda i, j: (i, j)),
                pl.BlockSpec((None, r // n_i, c // n_j), lambda i, j: (j, i, 0)),
                (n_j, r, c // n_j), True)
    if r % (n_i * n_j * BF16_SUBLANES) == 0:
        return same(pl.BlockSpec((r // (n_i * n_j), c), lambda i, j: (i * n_j + j, 0)), True)
    if r % (n_i * BF16_SUBLANES) == 0 and c % (n_j * LANES) == 0:
        return same(pl.BlockSpec((r // n_i, c // n_j), lambda i, j: (i, j)), True)
    if r % (n_j * BF16_SUBLANES) == 0 and c % (n_i * LANES) == 0:
        return same(pl.BlockSpec((r // n_j, c // n_i), lambda i, j: (j, i)), True)
    if r % (n_i * BF16_SUBLANES) == 0:
        return same(pl.BlockSpec((r // n_i, c), lambda i, j: (i, 0)), False)
    rows = next(k for k in range(BF16_SUBLANES, r + 1, BF16_SUBLANES)
                if r % k == 0 and r // k <= n_i * n_j)
    last = r // rows - 1
    return same(pl.BlockSpec((rows, c), lambda i, j: (jnp.minimum(i * n_j + j, last), 0)), True)


def _cast_side_args(cast, n_i, n_j):
    plans = [_cast_plan(w.shape, n_i, n_j, blocked) for w, blocked in cast]
    order = ([k for k, p in enumerate(plans) if p[3]] + [k for k, p in enumerate(plans) if not p[3]])
    in_specs = [plans[k][0] for k in order]
    out_specs = [plans[k][1] for k in order]
    out_shapes = [jax.ShapeDtypeStruct(plans[k][2], BF16) for k in order]
    n_every = sum(1 for p in plans if p[3])
    return order, in_specs, out_specs, out_shapes, n_every


def _unpermute(values, order):
    out = [None] * len(order)
    for pos, k in enumerate(order):
        out[k] = values[pos]
    return out


def _cast_blocks(pairs):
    for src_ref, dst_ref in pairs:
        dst_ref[...] = src_ref[...].astype(BF16)


def _ffn_kernel(*refs, n_f, final_norm, own_cast, has_lead, n_cast, n_cast_every):
    n_in = 6 if final_norm else 5
    n_own = 3 if own_cast else 0
    x_ref, g_ref = refs[:2]
    w_refs = refs[2:5]
    gf_ref = refs[5] if final_norm else None
    lead_ref = refs[n_in + n_cast] if has_lead else None
    n_inputs = n_in + n_cast + (1 if has_lead else 0)
    o_ref = refs[n_inputs]
    own_refs = refs[n_inputs + 1:n_inputs + 1 + n_own]
    casts = list(zip(refs[n_in:n_in + n_cast],
                     refs[n_inputs + 1 + n_own:n_inputs + 1 + n_own + n_cast]))
    cast_every, cast_first = casts[:n_cast_every], casts[n_cast_every:]
    h_ref = refs[n_inputs + 1 + n_own + n_cast]
    i = pl.program_id(0)
    j = pl.program_id(1)

    def partial_down(h):
        if own_cast:
            _cast_blocks(zip(w_refs, own_refs))
        wg_ref, wu_ref, wd_ref = own_refs if own_cast else w_refs
        gate = jnp.dot(h, wg_ref[...], preferred_element_type=F32)
        up = jnp.dot(h, wu_ref[...], preferred_element_type=F32)
        act = (_silu(gate) * up).astype(BF16)
        return jnp.dot(act, wd_ref[...], preferred_element_type=F32)

    def when(cond):
        return pl.when(cond & (i > 0)) if has_lead else pl.when(cond)

    @when(j == 0)
    def _():
        _cast_blocks(cast_every + cast_first)
        h = _rmsnorm(x_ref[...], g_ref[...]).astype(BF16)
        h_ref[...] = h
        o_ref[...] = partial_down(h)

    @when((j > 0) & (j < n_f - 1))
    def _():
        _cast_blocks(cast_every)
        o_ref[...] += partial_down(h_ref[...])

    @when(j == n_f - 1)
    def _():
        _cast_blocks(cast_every)
        y = x_ref[...] + 0.5 * (o_ref[...] + partial_down(h_ref[...]))
        if final_norm:
            y = _rmsnorm(y, gf_ref[...])
        o_ref[...] = y

    if has_lead:
        @pl.when((i == 0) & (j == 0))
        def _():
            _cast_blocks(cast_every + cast_first)
            copy = pltpu.make_async_copy(lead_ref, o_ref, refs[-1])
            copy.start()
            copy.wait()

        @pl.when((i == 0) & (j > 0))
        def _():
            _cast_blocks(cast_every)


def _ffn(x, g, wg, wu, wd, g_final=None, *, tm, tf, cast=(), n_tiles=None, lead=None):
    t, d = x.shape
    n_i = t // tm if n_tiles is None else n_tiles
    n_f = wd.shape[0] // tf
    assert n_f >= 2
    final_norm = g_final is not None
    own_cast = wg.dtype == F32
    has_lead = lead is not None
    assert not (own_cast and has_lead)

    def fstep(i, j):
        return jnp.where(i == 0, 0, j) if has_lead else j

    def col_tile(w):
        if w.ndim == 3:
            return pl.BlockSpec((None, d, tf), lambda i, j: (fstep(i, j), 0, 0))
        return pl.BlockSpec((d, tf), lambda i, j: (0, fstep(i, j)))

    w_specs = [col_tile(wg), col_tile(wu), pl.BlockSpec((tf, d), lambda i, j: (fstep(i, j), 0))]
    row_spec = pl.BlockSpec((tm, d), lambda i, j: (i, 0))
    in_specs = [row_spec, pl.BlockSpec((1, d), lambda i, j: (0, 0))] + w_specs
    args = [x, g, wg, wu, wd]
    if final_norm:
        in_specs.append(pl.BlockSpec((1, d), lambda i, j: (0, 0)))
        args.append(g_final)
    order, cast_in, cast_out, cast_shapes, n_every = _cast_side_args(cast, n_i, n_f)
    in_specs += cast_in
    args += [cast[k][0] for k in order]
    scratch = [pltpu.VMEM((tm, d), BF16)]
    if has_lead:
        assert lead.shape == (tm, d)
        in_specs.append(pl.BlockSpec(memory_space=pl.ANY))
        args.append(lead)
        scratch.append(pltpu.SemaphoreType.DMA(()))
    own_specs = w_specs if own_cast else []
    own_shapes = [jax.ShapeDtypeStruct(w.shape, BF16) for w in (wg, wu, wd)] if own_cast else []
    outs = pl.pallas_call(
        functools.partial(_ffn_kernel, n_f=n_f, final_norm=final_norm, own_cast=own_cast,
                          has_lead=has_lead, n_cast=len(cast), n_cast_every=n_every),
        grid=(n_i, n_f),
        in_specs=in_specs,
        out_specs=[row_spec] + own_specs + cast_out,
        out_shape=[jax.ShapeDtypeStruct((n_i * tm, d), F32)] + own_shapes + cast_shapes,
        scratch_shapes=scratch,
        compiler_params=pltpu.CompilerParams(
            dimension_semantics=("arbitrary", "arbitrary"),
            vmem_limit_bytes=VMEM_LIMIT),
        name="ffn_final" if final_norm else ("ffn_lead" if own_cast else "ffn"),
    )(*args)
    n_own = len(own_shapes)
    return outs[0], _unpermute(outs[1 + n_own:], order), list(outs[1:1 + n_own])


def _in_proj_kernel(*refs, first_tile_scale, rot_tiles, n_cast, n_cast_every):
    x_ref, g_ref, w_ref, cos_ref, sin_ref = refs[:5]
    o_ref = refs[5 + n_cast]
    h_ref = refs[-1]
    casts = list(zip(refs[5:5 + n_cast], refs[6 + n_cast:6 + 2 * n_cast]))
    cast_every, cast_first = casts[:n_cast_every], casts[n_cast_every:]
    j = pl.program_id(1)

    def project(h):
        return jnp.dot(h, w_ref[...], preferred_element_type=F32)

    @pl.when(j == 0)
    def _():
        _cast_blocks(cast_every + cast_first)
        h = _rmsnorm(x_ref[...], g_ref[...]).astype(BF16)
        h_ref[...] = h
        o_ref[...] = (project(h) * first_tile_scale).astype(BF16)

    plain = j > 0
    for tile, scale in rot_tiles:
        plain = plain & (j != tile)

        @pl.when(j == tile)
        def _(scale=scale):
            _cast_blocks(cast_every)
            acc = project(h_ref[...])
            cos = cos_ref[...]
            sin = sin_ref[...]
            for hd in range(o_ref.shape[1] // HEAD_DIM):
                cols = slice(hd * HEAD_DIM, (hd + 1) * HEAD_DIM)
                xh = acc[:, cols]
                rot = xh * cos + pltpu.roll(xh, HEAD_DIM // 2, axis=1) * sin
                o_ref[:, cols] = (rot * scale).astype(BF16)

    @pl.when(plain)
    def _():
        _cast_blocks(cast_every)
        o_ref[...] = project(h_ref[...]).astype(BF16)


def _in_proj(x, g, w, cos2, sin2, *, first_tile_scale, rot_tiles, cast=(), tm=1024, tn=WIDTH):
    t, d = x.shape
    n_j = w.shape[0]
    assert w.shape[1:] == (d, tn)
    n_i = t // tm
    seq_tiles = cos2.shape[0] // tm
    order, cast_in, cast_out, cast_shapes, n_every = _cast_side_args(cast, n_i, n_j)
    table = pl.BlockSpec((tm, HEAD_DIM), lambda i, j: (lax.rem(i, seq_tiles), 0))
    outs = pl.pallas_call(
        functools.partial(_in_proj_kernel, first_tile_scale=first_tile_scale, rot_tiles=rot_tiles,
                          n_cast=len(cast), n_cast_every=n_every),
        grid=(n_i, n_j),
        in_specs=[
            pl.BlockSpec((tm, d), lambda i, j: (i, 0)),
            pl.BlockSpec((1, d), lambda i, j: (0, 0)),
            pl.BlockSpec((None, d, tn), lambda i, j: (j, 0, 0)),
            table, table,
        ] + cast_in,
        out_specs=[pl.BlockSpec((None, tm, tn), lambda i, j: (j, i, 0))] + cast_out,
        out_shape=[jax.ShapeDtypeStruct((n_j, t, tn), BF16)] + cast_shapes,
        scratch_shapes=[pltpu.VMEM((tm, d), BF16)],
        compiler_params=pltpu.CompilerParams(
            dimension_semantics=("arbitrary", "arbitrary"),
            vmem_limit_bytes=VMEM_LIMIT),
        name="in_proj",
    )(x, g, w, cos2, sin2, *[cast[k][0] for k in order])
    return outs[0], _unpermute(outs[1:], order)


_LOG_GAMMA = tuple(math.log(1.0 - 2.0 ** (-5.0 - h)) for h in range(HEADS))


def _dot_nt(a, b):
    return lax.dot_general(a, b, (((1,), (1,)), ((), ())), preferred_element_type=F32)


def _lane_slabs(x):
    return [x[:, k:k + LANES] for k in range(0, x.shape[1], LANES)]


def _dot_tn(a, b):
    return lax.dot_general(a, b, (((0,), (0,)), ((), ())), preferred_element_type=F32)


def _mixer_out_kernel(qa_ref, ka0_ref, ka1_ref, ka2_ref, va0_ref, va1_ref, va2_ref,
                      qr_ref, kr_ref, vr_ref, gr_ref, ga0_ref, ga1_ref, gm0_ref, gm1_ref,
                      relrow_ref, xprev_ref, woa_ref, wor_ref, wo_ref,
                      o_ref,
                      bias_ref, decay_ref, state_ref,
                      att_ref, ret_ref, attp_ref, retp_ref, merged_ref, *, n_tiles, tiles_per_seq):
    s = pl.program_id(0)
    t = jnp.minimum(s, n_tiles - 1)
    i = lax.rem(t, tiles_per_seq)
    ka_refs = (ka0_ref, ka1_ref, ka2_ref)
    va_refs = (va0_ref, va1_ref, va2_ref)

    @pl.when(s == 0)
    def _():
        qc = lax.broadcasted_iota(jnp.int32, (TQ, N_KBLK * TQ), 0) >> CHUNK_SHIFT
        kc = lax.broadcasted_iota(jnp.int32, (TQ, N_KBLK * TQ), 1) >> CHUNK_SHIFT
        in_band = (kc >= qc) & (kc <= qc + N_PREV_CHUNKS)
        for h in range(HEADS):
            rows = jnp.broadcast_to(relrow_ref[h:h + 1, :], (TQ, (N_KBLK + 1) * TQ))
            toeplitz = pltpu.roll(rows, 0, axis=1, stride=1, stride_axis=0)
            bias_ref[h] = jnp.where(in_band, toeplitz[:, TQ:] * LOG2E, NEG)
        bias_ref[HEADS] = jnp.full((TQ, N_KBLK * TQ), NEG, F32)

        n = lax.broadcasted_iota(jnp.int32, (TQ, TQ), 0)
        m = lax.broadcasted_iota(jnp.int32, (TQ, TQ), 1)
        dist = jnp.abs(n - m).astype(F32)
        visible = (m >> CHUNK_SHIFT) <= (n >> CHUNK_SHIFT)
        for h in range(HEADS):
            decay_ref[h] = jnp.where(visible, jnp.exp(_LOG_GAMMA[h] * dist), 0.0)
        att_ref[...] = jnp.zeros_like(att_ref)
        ret_ref[...] = jnp.zeros_like(ret_ref)

    @pl.when(i == 0)
    def _():
        state_ref[...] = jnp.zeros_like(state_ref)

    attp_ref[...] = att_ref[...]
    retp_ref[...] = ret_ref[...]

    row = lax.broadcasted_iota(jnp.int32, (TQ, HEAD_DIM), 0).astype(F32)

    def scores(h):
        cols = slice(h * HEAD_DIM, (h + 1) * HEAD_DIM)
        q = qa_ref[:, cols]
        sc = []
        for j in range(N_KBLK):
            plane = h if j == N_KBLK - 1 else jnp.where(i + j >= N_KBLK - 1, h, HEADS)
            sc.append(_dot_nt(q, ka_refs[j][:, cols]) + bias_ref[plane, :, j * TQ:(j + 1) * TQ])
        return sc

    def attend(h, sc):
        cols = slice(h * HEAD_DIM, (h + 1) * HEAD_DIM)
        mx = jnp.maximum(jnp.maximum(jnp.max(sc[0], axis=-1, keepdims=True),
                                     jnp.max(sc[1], axis=-1, keepdims=True)),
                         jnp.max(sc[2], axis=-1, keepdims=True))
        acc = None
        den = None
        for j in range(N_KBLK):
            p = jnp.exp2(sc[j] - mx)
            dj = jnp.sum(p, axis=-1, keepdims=True)
            oj = jnp.dot(p.astype(BF16), va_refs[j][:, cols], preferred_element_type=F32)
            acc = oj if acc is None else acc + oj
            den = dj if den is None else den + dj
        att_ref[:, cols] = (acc / den).astype(BF16)

    def retain(h, fill):
        cols = slice(h * HEAD_DIM, (h + 1) * HEAD_DIM)
        qr = qr_ref[:, cols]
        kr = kr_ref[:, cols]
        kr_f = kr.astype(F32)
        v = vr_ref[:, cols]
        sr = _dot_nt(qr, kr)
        st = state_ref[h]
        lg = _LOG_GAMMA[h]
        cross = jnp.dot(qr, st.astype(BF16), preferred_element_type=F32)
        fill()
        inner = jnp.dot((sr * decay_ref[h]).astype(BF16), v, preferred_element_type=F32)
        kd = (kr_f * jnp.exp(lg * (TQ - 1.0 - row))).astype(BF16)
        state_ref[h] = st * math.exp(lg * TQ) + _dot_tn(kd, v)
        out = inner + cross * jnp.exp(lg * (row + 1.0))
        out = out * lax.rsqrt(jnp.mean(out * out, axis=-1, keepdims=True) + EPS)
        ret_ref[:, cols] = (_silu(gr_ref[:, cols].astype(F32)) * out).astype(BF16)

    gate_refs = ((ga0_ref, gm0_ref), (ga1_ref, gm1_ref))

    def merge_piece(k):
        cols = slice(k * OUT_PIECE, (k + 1) * OUT_PIECE)
        ga_ref, gm_ref = gate_refs[k * OUT_PIECE // WIDTH]
        gcols = slice(k * OUT_PIECE % WIDTH, k * OUT_PIECE % WIDTH + OUT_PIECE)
        branch_a = jnp.dot(attp_ref[...], woa_ref[:, cols], preferred_element_type=F32)
        branch_r = jnp.dot(retp_ref[...], wor_ref[:, cols], preferred_element_type=F32)
        merged = (jax.nn.sigmoid(ga_ref[:, gcols].astype(F32)) * branch_a
                  + jax.nn.sigmoid(gm_ref[:, gcols].astype(F32)) * branch_r)
        merged_ref[:, cols] = merged.astype(BF16)

    def out_piece(k):
        cols = slice(k * OUT_PIECE, (k + 1) * OUT_PIECE)
        o_ref[:, cols] = xprev_ref[:, cols] + jnp.dot(merged_ref[...], wo_ref[:, cols],
                                                      preferred_element_type=F32)

    n_pieces = merged_ref.shape[1] // OUT_PIECE
    pieces = ([functools.partial(merge_piece, k) for k in range(n_pieces)]
              + [functools.partial(out_piece, k) for k in range(n_pieces)])
    assert len(pieces) == 2 * HEADS
    sc = scores(0)
    for h in range(HEADS):
        sc_next = scores(h + 1) if h + 1 < HEADS else None
        pieces[2 * h]()
        attend(h, sc)
        retain(h, pieces[2 * h + 1])
        sc = sc_next


def _mixer_out(proj, x, rel_row, woa, wor, wo, batch, seq):
    t, d = x.shape
    tiles_per_seq = seq // TQ
    n_tiles = batch * tiles_per_seq
    proj4 = proj.reshape(proj.shape[0], batch, seq, WIDTH)

    def split(tt):
        return tt // tiles_per_seq, lax.rem(tt, tiles_per_seq)

    def cur(s):
        return split(jnp.minimum(s, n_tiles - 1))

    def tile(col):
        return pl.BlockSpec((None, None, TQ, WIDTH), lambda s: (col, *cur(s), 0))

    def prev_tile(col):
        return pl.BlockSpec((None, None, TQ, WIDTH),
                            lambda s: (col, *split(jnp.maximum(s - 1, 0)), 0))

    def hist(col, j):
        def index(s):
            b, i = cur(s)
            return col, b, jnp.maximum(i - (N_KBLK - 1) + j, 0), 0
        return pl.BlockSpec((None, None, TQ, WIDTH), index)

    def resident(shape):
        return pl.BlockSpec(shape, lambda s: (0, 0), pipeline_mode=pl.Buffered(1))

    prev = pl.BlockSpec((TQ, d), lambda s: (jnp.maximum(s - 1, 0), 0))
    gate0 = 7
    in_specs = ([tile(0)] + [hist(1, j) for j in range(N_KBLK)] + [hist(2, j) for j in range(N_KBLK)]
                + [tile(3), tile(4), tile(5), tile(6)]
                + [prev_tile(gate0 + c) for c in range(4)]
                + [resident(rel_row.shape), prev,
                   resident(woa.shape), resident(wor.shape), resident(wo.shape)])
    return pl.pallas_call(
        functools.partial(_mixer_out_kernel, n_tiles=n_tiles, tiles_per_seq=tiles_per_seq),
        grid=(n_tiles + 1,),
        in_specs=in_specs,
        out_specs=prev,
        out_shape=jax.ShapeDtypeStruct((t, d), F32),
        scratch_shapes=[pltpu.VMEM((HEADS + 1, TQ, N_KBLK * TQ), F32),
                        pltpu.VMEM((HEADS, TQ, TQ), F32),
                        pltpu.VMEM((HEADS, HEAD_DIM, HEAD_DIM), F32),
                        pltpu.VMEM((TQ, WIDTH), BF16),
                        pltpu.VMEM((TQ, WIDTH), BF16),
                        pltpu.VMEM((TQ, WIDTH), BF16),
                        pltpu.VMEM((TQ, WIDTH), BF16),
                        pltpu.VMEM((TQ, d), BF16)],
        compiler_params=pltpu.CompilerParams(
            dimension_semantics=("arbitrary",),
            vmem_limit_bytes=VMEM_LIMIT),
        name="mixer_out",
    )(*([proj4] * 15), rel_row, x, woa, wor, wo)


def _rotary_tables(seq):
    inv = 1.0 / (ROPE_BASE ** (np.arange(0, HEAD_DIM, 2, dtype=np.float64) / HEAD_DIM))
    ang = np.arange(seq, dtype=np.float64)[:, None] * inv[None, :]
    cos, sin = np.cos(ang), np.sin(ang)
    return (np.concatenate([cos, cos], axis=-1).astype(np.float32),
            np.concatenate([-sin, sin], axis=-1).astype(np.float32))


def _rel_row(rel_bias):
    j = np.arange((N_KBLK + 1) * TQ)
    rel_idx = np.clip(N_KBLK * TQ - j, -MAX_REL_DIST, MAX_REL_DIST) + MAX_REL_DIST
    return rel_bias.astype(F32)[:, rel_idx]


def kernel(x, norm_ffn1_g, ffn1_w_gate, ffn1_w_up, ffn1_w_down, norm_mix_g, w_in, rel_bias,
           w_out_att, w_out_ret, w_out, norm_ffn2_g, ffn2_w_gate, ffn2_w_up, ffn2_w_down,
           norm_final_g):
    batch, seq, d = x.shape
    depth = w_in.shape[0]
    assert depth >= 1 and (N_KBLK - 1) * TQ == N_PREV_CHUNKS * CHUNK and seq % TQ == 0
    cos2, sin2 = _rotary_tables(seq)
    xt = x.reshape(batch * seq, d)
    for l in range(depth):
        last = l == depth - 1
        g1 = norm_ffn1_g[l][None]
        y_lead, _, (wg1, wu1, wd1) = _ffn(xt, g1, ffn1_w_gate[l], ffn1_w_up[l], ffn1_w_down[l],
                                          n_tiles=1, tm=FFN_TM, tf=FFN_TF // 2)
        xt, (w_in_b, wg2, wu2, wd2), _ = _ffn(
            xt, g1, wg1, wu1, wd1, lead=y_lead, tm=FFN_TM, tf=FFN_TF,
            cast=((w_in[l], True), (ffn2_w_gate[l], True), (ffn2_w_up[l], True),
                  (ffn2_w_down[l], False)))
        proj, (woa, wor, wo) = _in_proj(
            xt, norm_mix_g[l][None], w_in_b, cos2, sin2, first_tile_scale=HEAD_DIM ** -0.5 * LOG2E,
            rot_tiles=((3, HEAD_DIM ** -0.5), (4, 1.0)),
            cast=((w_out_att[l], False), (w_out_ret[l], False), (w_out[l], False)))
        xt = _mixer_out(proj, xt, _rel_row(rel_bias[l]), woa, wor, wo, batch, seq)
        xt, _, _ = _ffn(xt, norm_ffn2_g[l][None], wg2, wu2, wd2, norm_final_g[None] if last else None,
                        tm=FFN_TM, tf=FFN_TF)
    return xt.reshape(batch, seq, d)
```

```python
import functools
import math

import numpy as np
import jax
import jax.numpy as jnp
from jax import lax
from jax.experimental import pallas as pl
from jax.experimental.pallas import tpu as pltpu

F32 = jnp.float32
BF16 = jnp.bfloat16

CHUNK = 64
CHUNK_SHIFT = CHUNK.bit_length() - 1
N_PREV_CHUNKS = 8
HEADS = 8
HEAD_DIM = 128
WIDTH = HEADS * HEAD_DIM
MAX_REL_DIST = 128
ROPE_BASE = 10000.0
EPS = 1e-6
NEG = -1e30
LOG2E = math.log2(math.e)

LANES = 128
BF16_SUBLANES = 16

TQ = 256
N_KBLK = 3
FFN_TM = 1024
FFN_TF = 512
OUT_PIECE = 256
MIB = 1024 * 1024
VMEM_LIMIT = 60 * MIB


def _rmsnorm(x, g):
    ms = jnp.mean(x * x, axis=-1, keepdims=True)
    return x * lax.rsqrt(ms + EPS) * g


def _silu(x):
    return x * jax.nn.sigmoid(x)


def _cast_plan(shape, n_i, n_j, blocked):
    r, c = shape
    same = lambda spec, every: (spec, spec, shape, every)
    if blocked:
        assert r % (n_i * BF16_SUBLANES) == 0 and c % (n_j * LANES) == 0, shape
        return (pl.BlockSpec((r // n_i, c // n_j), lambda i, j: (i, j)),
                pl.BlockSpec((None, r // n_i, c // n_j), lambda i, j: (j, i, 0)),
                (n_j, r, c // n_j), True)
    if r % (n_i * n_j * BF16_SUBLANES) == 0:
        return same(pl.BlockSpec((r // (n_i * n_j), c), lambda i, j: (i * n_j + j, 0)), True)
    if r % (n_i * BF16_SUBLANES) == 0 and c % (n_j * LANES) == 0:
        return same(pl.BlockSpec((r // n_i, c // n_j), lambda i, j: (i, j)), True)
    if r % (n_j * BF16_SUBLANES) == 0 and c % (n_i * LANES) == 0:
        return same(pl.BlockSpec((r // n_j, c // n_i), lambda i, j: (j, i)), True)
    if r % (n_i * BF16_SUBLANES) == 0:
        return same(pl.BlockSpec((r // n_i, c), lambda i, j: (i, 0)), False)
    rows = next(k for k in range(BF16_SUBLANES, r + 1, BF16_SUBLANES)
                if r % k == 0 and r // k <= n_i * n_j)
    last = r // rows - 1
    return same(pl.BlockSpec((rows, c), lambda i, j: (jnp.minimum(i * n_j + j, last), 0)), True)


def _cast_side_args(cast, n_i, n_j):
    plans = [_cast_plan(w.shape, n_i, n_j, blocked) for w, blocked in cast]
    order = ([k for k, p in enumerate(plans) if p[3]] + [k for k, p in enumerate(plans) if not p[3]])
    in_specs = [plans[k][0] for k in order]
    out_specs = [plans[k][1] for k in order]
    out_shapes = [jax.ShapeDtypeStruct(plans[k][2], BF16) for k in order]
    n_every = sum(1 for p in plans if p[3])
    return order, in_specs, out_specs, out_shapes, n_every


def _unpermute(values, order):
    out = [None] * len(order)
    for pos, k in enumerate(order):
        out[k] = values[pos]
    return out


def _cast_blocks(pairs):
    for src_ref, dst_ref in pairs:
        dst_ref[...] = src_ref[...].astype(BF16)


def _dot_col_tiles(h, w_ref):
    if len(w_ref.shape) == 2:
        return jnp.dot(h, w_ref[...], preferred_element_type=F32)
    return jnp.concatenate([jnp.dot(h, w_ref[q], preferred_element_type=F32)
                            for q in range(w_ref.shape[0])], axis=1)


def _ffn_kernel(*refs, n_f, final_norm, own_cast, has_lead, n_cast, n_cast_every):
    n_in = 6 if final_norm else 5
    n_own = 3 if own_cast else 0
    x_ref, g_ref = refs[:2]
    w_refs = refs[2:5]
    gf_ref = refs[5] if final_norm else None
    lead_ref = refs[n_in + n_cast] if has_lead else None
    n_inputs = n_in + n_cast + (1 if has_lead else 0)
    o_ref = refs[n_inputs]
    own_refs = refs[n_inputs + 1:n_inputs + 1 + n_own]
    casts = list(zip(refs[n_in:n_in + n_cast],
                     refs[n_inputs + 1 + n_own:n_inputs + 1 + n_own + n_cast]))
    cast_every, cast_first = casts[:n_cast_every], casts[n_cast_every:]
    h_ref = refs[n_inputs + 1 + n_own + n_cast]
    i = pl.program_id(0)
    j = pl.program_id(1)

    def partial_down(h):
        if own_cast:
            _cast_blocks(zip(w_refs, own_refs))
        wg_ref, wu_ref, wd_ref = own_refs if own_cast else w_refs
        gate = _dot_col_tiles(h, wg_ref)
        up = _dot_col_tiles(h, wu_ref)
        act = (_silu(gate) * up).astype(BF16)
        return jnp.dot(act, wd_ref[...], preferred_element_type=F32)

    def when(cond):
        return pl.when(cond & (i > 0)) if has_lead else pl.when(cond)

    @when(j == 0)
    def _():
        _cast_blocks(cast_every + cast_first)
        h = _rmsnorm(x_ref[...], g_ref[...]).astype(BF16)
        h_ref[...] = h
        o_ref[...] = partial_down(h)

    @when((j > 0) & (j < n_f - 1))
    def _():
        _cast_blocks(cast_every)
        o_ref[...] += partial_down(h_ref[...])

    @when(j == n_f - 1)
    def _():
        _cast_blocks(cast_every)
        y = x_ref[...] + 0.5 * (o_ref[...] + partial_down(h_ref[...]))
        if final_norm:
            y = _rmsnorm(y, gf_ref[...])
        o_ref[...] = y

    if has_lead:
        @pl.when((i == 0) & (j == 0))
        def _():
            _cast_blocks(cast_every + cast_first)
            copy = pltpu.make_async_copy(lead_ref, o_ref, refs[-1])
            copy.start()
            copy.wait()

        @pl.when((i == 0) & (j > 0))
        def _():
            _cast_blocks(cast_every)


def _ffn(x, g, wg, wu, wd, g_final=None, *, tm, tf, cast=(), n_tiles=None, lead=None):
    t, d = x.shape
    n_i = t // tm if n_tiles is None else n_tiles
    n_f = wd.shape[0] // tf
    assert n_f >= 2
    final_norm = g_final is not None
    own_cast = wg.dtype == F32
    has_lead = lead is not None
    assert not (own_cast and has_lead)

    def fstep(i, j):
        return jnp.where(i == 0, 0, j) if has_lead else j

    def col_tile(w):
        if w.ndim == 2:
            return pl.BlockSpec((d, tf), lambda i, j: (0, fstep(i, j)))
        if w.shape[2] == tf:
            return pl.BlockSpec((None, d, tf), lambda i, j: (fstep(i, j), 0, 0))
        return pl.BlockSpec((tf // w.shape[2], d, w.shape[2]), lambda i, j: (fstep(i, j), 0, 0))

    w_specs = [col_tile(wg), col_tile(wu), pl.BlockSpec((tf, d), lambda i, j: (fstep(i, j), 0))]
    row_spec = pl.BlockSpec((tm, d), lambda i, j: (i, 0))
    in_specs = [row_spec, pl.BlockSpec((1, d), lambda i, j: (0, 0))] + w_specs
    args = [x, g, wg, wu, wd]
    if final_norm:
        in_specs.append(pl.BlockSpec((1, d), lambda i, j: (0, 0)))
        args.append(g_final)
    order, cast_in, cast_out, cast_shapes, n_every = _cast_side_args(cast, n_i, n_f)
    in_specs += cast_in
    args += [cast[k][0] for k in order]
    scratch = [pltpu.VMEM((tm, d), BF16)]
    if has_lead:
        assert lead.shape == (tm, d)
        in_specs.append(pl.BlockSpec(memory_space=pl.ANY))
        args.append(lead)
        scratch.append(pltpu.SemaphoreType.DMA(()))
    own_specs, own_shapes = [], []
    if own_cast:
        blocked = pl.BlockSpec((None, d, tf), lambda i, j: (j, 0, 0))
        own_specs = [blocked, blocked, w_specs[2]]
        own_shapes = [jax.ShapeDtypeStruct(shape, BF16)
                      for shape in ((n_f, d, tf), (n_f, d, tf), wd.shape)]
    outs = pl.pallas_call(
        functools.partial(_ffn_kernel, n_f=n_f, final_norm=final_norm, own_cast=own_cast,
                          has_lead=has_lead, n_cast=len(cast), n_cast_every=n_every),
        grid=(n_i, n_f),
        in_specs=in_specs,
        out_specs=[row_spec] + own_specs + cast_out,
        out_shape=[jax.ShapeDtypeStruct((n_i * tm, d), F32)] + own_shapes + cast_shapes,
        scratch_shapes=scratch,
        compiler_params=pltpu.CompilerParams(
            dimension_semantics=("arbitrary", "arbitrary"),
            vmem_limit_bytes=VMEM_LIMIT),
        name="ffn_final" if final_norm else ("ffn_lead" if own_cast else "ffn"),
    )(*args)
    n_own = len(own_shapes)
    return outs[0], _unpermute(outs[1 + n_own:], order), list(outs[1:1 + n_own])


def _in_proj_kernel(*refs, first_tile_scale, rot_tiles, n_cast, n_cast_every):
    x_ref, g_ref, w_ref, cos_ref, sin_ref = refs[:5]
    o_ref = refs[5 + n_cast]
    h_ref = refs[-1]
    casts = list(zip(refs[5:5 + n_cast], refs[6 + n_cast:6 + 2 * n_cast]))
    cast_every, cast_first = casts[:n_cast_every], casts[n_cast_every:]
    j = pl.program_id(1)

    def project(h):
        return jnp.dot(h, w_ref[...], preferred_element_type=F32)

    @pl.when(j == 0)
    def _():
        _cast_blocks(cast_every + cast_first)
        h = _rmsnorm(x_ref[...], g_ref[...]).astype(BF16)
        h_ref[...] = h
        o_ref[...] = (project(h) * first_tile_scale).astype(BF16)

    plain = j > 0
    for tile, scale in rot_tiles:
        plain = plain & (j != tile)

        @pl.when(j == tile)
        def _(scale=scale):
            _cast_blocks(cast_every)
            acc = project(h_ref[...])
            cos = cos_ref[...]
            sin = sin_ref[...]
            for hd in range(o_ref.shape[1] // HEAD_DIM):
                cols = slice(hd * HEAD_DIM, (hd + 1) * HEAD_DIM)
                xh = acc[:, cols]
                rot = xh * cos + pltpu.roll(xh, HEAD_DIM // 2, axis=1) * sin
                o_ref[:, cols] = (rot * scale).astype(BF16)

    @pl.when(plain)
    def _():
        _cast_blocks(cast_every)
        o_ref[...] = project(h_ref[...]).astype(BF16)


def _in_proj(x, g, w, cos2, sin2, *, first_tile_scale, rot_tiles, cast=(), tm=1024, tn=WIDTH):
    t, d = x.shape
    n_j = w.shape[0]
    assert w.shape[1:] == (d, tn)
    n_i = t // tm
    seq_tiles = cos2.shape[0] // tm
    order, cast_in, cast_out, cast_shapes, n_every = _cast_side_args(cast, n_i, n_j)
    table = pl.BlockSpec((tm, HEAD_DIM), lambda i, j: (lax.rem(i, seq_tiles), 0))
    outs = pl.pallas_call(
        functools.partial(_in_proj_kernel, first_tile_scale=first_tile_scale, rot_tiles=rot_tiles,
                          n_cast=len(cast), n_cast_every=n_every),
        grid=(n_i, n_j),
        in_specs=[
            pl.BlockSpec((tm, d), lambda i, j: (i, 0)),
            pl.BlockSpec((1, d), lambda i, j: (0, 0)),
            pl.BlockSpec((None, d, tn), lambda i, j: (j, 0, 0)),
            table, table,
        ] + cast_in,
        out_specs=[pl.BlockSpec((None, tm, tn), lambda i, j: (j, i, 0))] + cast_out,
        out_shape=[jax.ShapeDtypeStruct((n_j, t, tn), BF16)] + cast_shapes,
        scratch_shapes=[pltpu.VMEM((tm, d), BF16)],
        compiler_params=pltpu.CompilerParams(
            dimension_semantics=("arbitrary", "arbitrary"),
            vmem_limit_bytes=VMEM_LIMIT),
        name="in_proj",
    )(x, g, w, cos2, sin2, *[cast[k][0] for k in order])
    return outs[0], _unpermute(outs[1:], order)


_LOG_GAMMA = tuple(math.log(1.0 - 2.0 ** (-5.0 - h)) for h in range(HEADS))


def _dot_nt(a, b):
    return lax.dot_general(a, b, (((1,), (1,)), ((), ())), preferred_element_type=F32)


def _lane_slabs(x):
    return [x[:, k:k + LANES] for k in range(0, x.shape[1], LANES)]


def _dot_tn(a, b):
    return lax.dot_general(a, b, (((0,), (0,)), ((), ())), preferred_element_type=F32)


def _mixer_out_kernel(qa_ref, ka0_ref, ka1_ref, ka2_ref, va0_ref, va1_ref, va2_ref,
                      qr_ref, kr_ref, vr_ref, gr_ref, ga0_ref, ga1_ref, gm0_ref, gm1_ref,
                      relrow_ref, xprev_ref, woa_ref, wor_ref, wo_ref,
                      o_ref,
                      bias_ref, decay_ref, state_ref,
                      att_ref, ret_ref, attp_ref, retp_ref, merged_ref, *, n_tiles, tiles_per_seq):
    s = pl.program_id(0)
    t = jnp.minimum(s, n_tiles - 1)
    i = lax.rem(t, tiles_per_seq)
    ka_refs = (ka0_ref, ka1_ref, ka2_ref)
    va_refs = (va0_ref, va1_ref, va2_ref)

    @pl.when(s == 0)
    def _():
        qc = lax.broadcasted_iota(jnp.int32, (TQ, N_KBLK * TQ), 0) >> CHUNK_SHIFT
        kc = lax.broadcasted_iota(jnp.int32, (TQ, N_KBLK * TQ), 1) >> CHUNK_SHIFT
        in_band = (kc >= qc) & (kc <= qc + N_PREV_CHUNKS)
        for h in range(HEADS):
            rows = jnp.broadcast_to(relrow_ref[h:h + 1, :], (TQ, (N_KBLK + 1) * TQ))
            toeplitz = pltpu.roll(rows, 0, axis=1, stride=1, stride_axis=0)
            bias_ref[h] = jnp.where(in_band, toeplitz[:, TQ:] * LOG2E, NEG)
        bias_ref[HEADS] = jnp.full((TQ, N_KBLK * TQ), NEG, F32)

        n = lax.broadcasted_iota(jnp.int32, (TQ, TQ), 0)
        m = lax.broadcasted_iota(jnp.int32, (TQ, TQ), 1)
        dist = jnp.abs(n - m).astype(F32)
        visible = (m >> CHUNK_SHIFT) <= (n >> CHUNK_SHIFT)
        for h in range(HEADS):
            decay_ref[h] = jnp.where(visible, jnp.exp(_LOG_GAMMA[h] * dist), 0.0)
        att_ref[...] = jnp.zeros_like(att_ref)
        ret_ref[...] = jnp.zeros_like(ret_ref)

    @pl.when(i == 0)
    def _():
        state_ref[...] = jnp.zeros_like(state_ref)

    attp_ref[...] = att_ref[...]
    retp_ref[...] = ret_ref[...]

    row = lax.broadcasted_iota(jnp.int32, (TQ, HEAD_DIM), 0).astype(F32)

    def scores(h):
        cols = slice(h * HEAD_DIM, (h + 1) * HEAD_DIM)
        q = qa_ref[:, cols]
        sc = []
        for j in range(N_KBLK):
            plane = h if j == N_KBLK - 1 else jnp.where(i + j >= N_KBLK - 1, h, HEADS)
            sc.append(_dot_nt(q, ka_refs[j][:, cols]) + bias_ref[plane, :, j * TQ:(j + 1) * TQ])
        return sc

    def attend(h, sc):
        cols = slice(h * HEAD_DIM, (h + 1) * HEAD_DIM)
        mx = jnp.maximum(jnp.maximum(jnp.max(sc[0], axis=-1, keepdims=True),
                                     jnp.max(sc[1], axis=-1, keepdims=True)),
                         jnp.max(sc[2], axis=-1, keepdims=True))
        acc = None
        den = None
        for j in range(N_KBLK):
            p = jnp.exp2(sc[j] - mx)
            dj = jnp.sum(p, axis=-1, keepdims=True)
            oj = jnp.dot(p.astype(BF16), va_refs[j][:, cols], preferred_element_type=F32)
            acc = oj if acc is None else acc + oj
            den = dj if den is None else den + dj
        att_ref[:, cols] = (acc / den).astype(BF16)

    def retain(h, fill):
        cols = slice(h * HEAD_DIM, (h + 1) * HEAD_DIM)
        qr = qr_ref[:, cols]
        kr = kr_ref[:, cols]
        kr_f = kr.astype(F32)
        v = vr_ref[:, cols]
        sr = _dot_nt(qr, kr)
        st = state_ref[h]
        lg = _LOG_GAMMA[h]
        cross = jnp.dot(qr, st.astype(BF16), preferred_element_type=F32)
        fill()
        inner = jnp.dot((sr * decay_ref[h]).astype(BF16), v, preferred_element_type=F32)
        kd = (kr_f * jnp.exp(lg * (TQ - 1.0 - row))).astype(BF16)
        state_ref[h] = st * math.exp(lg * TQ) + _dot_tn(kd, v)
        out = inner + cross * jnp.exp(lg * (row + 1.0))
        out = out * lax.rsqrt(jnp.mean(out * out, axis=-1, keepdims=True) + EPS)
        ret_ref[:, cols] = (_silu(gr_ref[:, cols].astype(F32)) * out).astype(BF16)

    gate_refs = ((ga0_ref, gm0_ref), (ga1_ref, gm1_ref))

    def merge_piece(k):
        cols = slice(k * OUT_PIECE, (k + 1) * OUT_PIECE)
        ga_ref, gm_ref = gate_refs[k * OUT_PIECE // WIDTH]
        gcols = slice(k * OUT_PIECE % WIDTH, k * OUT_PIECE % WIDTH + OUT_PIECE)
        branch_a = jnp.dot(attp_ref[...], woa_ref[:, cols], preferred_element_type=F32)
        branch_r = jnp.dot(retp_ref[...], wor_ref[:, cols], preferred_element_type=F32)
        merged = (jax.nn.sigmoid(ga_ref[:, gcols].astype(F32)) * branch_a
                  + jax.nn.sigmoid(gm_ref[:, gcols].astype(F32)) * branch_r)
        merged_ref[:, cols] = merged.astype(BF16)

    def out_piece(k):
        cols = slice(k * OUT_PIECE, (k + 1) * OUT_PIECE)
        o_ref[:, cols] = xprev_ref[:, cols] + jnp.dot(merged_ref[...], wo_ref[:, cols],
                                                      preferred_element_type=F32)

    n_pieces = merged_ref.shape[1] // OUT_PIECE
    pieces = ([functools.partial(merge_piece, k) for k in range(n_pieces)]
              + [functools.partial(out_piece, k) for k in range(n_pieces)])
    assert len(pieces) == 2 * HEADS
    sc = scores(0)
    for h in range(HEADS):
        sc_next = scores(h + 1) if h + 1 < HEADS else None
        pieces[2 * h]()
        attend(h, sc)
        retain(h, pieces[2 * h + 1])
        sc = sc_next


def _mixer_out(proj, x, rel_row, woa, wor, wo, batch, seq):
    t, d = x.shape
    tiles_per_seq = seq // TQ
    n_tiles = batch * tiles_per_seq
    proj4 = proj.reshape(proj.shape[0], batch, seq, WIDTH)

    def split(tt):
        return tt // tiles_per_seq, lax.rem(tt, tiles_per_seq)

    def cur(s):
        return split(jnp.minimum(s, n_tiles - 1))

    def tile(col):
        return pl.BlockSpec((None, None, TQ, WIDTH), lambda s: (col, *cur(s), 0))

    def prev_tile(col):
        return pl.BlockSpec((None, None, TQ, WIDTH),
                            lambda s: (col, *split(jnp.maximum(s - 1, 0)), 0))

    def hist(col, j):
        def index(s):
            b, i = cur(s)
            return col, b, jnp.maximum(i - (N_KBLK - 1) + j, 0), 0
        return pl.BlockSpec((None, None, TQ, WIDTH), index)

    def resident(shape):
        return pl.BlockSpec(shape, lambda s: (0, 0), pipeline_mode=pl.Buffered(1))

    prev = pl.BlockSpec((TQ, d), lambda s: (jnp.maximum(s - 1, 0), 0))
    gate0 = 7
    in_specs = ([tile(0)] + [hist(1, j) for j in range(N_KBLK)] + [hist(2, j) for j in range(N_KBLK)]
                + [tile(3), tile(4), tile(5), tile(6)]
                + [prev_tile(gate0 + c) for c in range(4)]
                + [resident(rel_row.shape), prev,
                   resident(woa.shape), resident(wor.shape), resident(wo.shape)])
    return pl.pallas_call(
        functools.partial(_mixer_out_kernel, n_tiles=n_tiles, tiles_per_seq=tiles_per_seq),
        grid=(n_tiles + 1,),
        in_specs=in_specs,
        out_specs=prev,
        out_shape=jax.ShapeDtypeStruct((t, d), F32),
        scratch_shapes=[pltpu.VMEM((HEADS + 1, TQ, N_KBLK * TQ), F32),
                        pltpu.VMEM((HEADS, TQ, TQ), F32),
                        pltpu.VMEM((HEADS, HEAD_DIM, HEAD_DIM), F32),
                        pltpu.VMEM((TQ, WIDTH), BF16),
                        pltpu.VMEM((TQ, WIDTH), BF16),
                        pltpu.VMEM((TQ, WIDTH), BF16),
                        pltpu.VMEM((TQ, WIDTH), BF16),
                        pltpu.VMEM((TQ, d), BF16)],
        compiler_params=pltpu.CompilerParams(
            dimension_semantics=("arbitrary",),
            vmem_limit_bytes=VMEM_LIMIT),
        name="mixer_out",
    )(*([proj4] * 15), rel_row, x, woa, wor, wo)


def _rotary_tables(seq):
    inv = 1.0 / (ROPE_BASE ** (np.arange(0, HEAD_DIM, 2, dtype=np.float64) / HEAD_DIM))
    ang = np.arange(seq, dtype=np.float64)[:, None] * inv[None, :]
    cos, sin = np.cos(ang), np.sin(ang)
    return (np.concatenate([cos, cos], axis=-1).astype(np.float32),
            np.concatenate([-sin, sin], axis=-1).astype(np.float32))


def _rel_row(rel_bias):
    j = np.arange((N_KBLK + 1) * TQ)
    rel_idx = np.clip(N_KBLK * TQ - j, -MAX_REL_DIST, MAX_REL_DIST) + MAX_REL_DIST
    return rel_bias.astype(F32)[:, rel_idx]


def kernel(x, norm_ffn1_g, ffn1_w_gate, ffn1_w_up, ffn1_w_down, norm_mix_g, w_in, rel_bias,
           w_out_att, w_out_ret, w_out, norm_ffn2_g, ffn2_w_gate, ffn2_w_up, ffn2_w_down,
           norm_final_g):
    batch, seq, d = x.shape
    depth = w_in.shape[0]
    assert depth >= 1 and (N_KBLK - 1) * TQ == N_PREV_CHUNKS * CHUNK and seq % TQ == 0
    cos2, sin2 = _rotary_tables(seq)
    xt = x.reshape(batch * seq, d)
    for l in range(depth):
        last = l == depth - 1
        g1 = norm_ffn1_g[l][None]
        y_lead, _, (wg1, wu1, wd1) = _ffn(xt, g1, ffn1_w_gate[l], ffn1_w_up[l], ffn1_w_down[l],
                                          n_tiles=1, tm=FFN_TM, tf=FFN_TF // 2)
        xt, (w_in_b, wg2, wu2, wd2), _ = _ffn(
            xt, g1, wg1, wu1, wd1, lead=y_lead, tm=FFN_TM, tf=FFN_TF,
            cast=((w_in[l], True), (ffn2_w_gate[l], True), (ffn2_w_up[l], True),
                  (ffn2_w_down[l], False)))
        proj, (woa, wor, wo) = _in_proj(
            xt, norm_mix_g[l][None], w_in_b, cos2, sin2, first_tile_scale=HEAD_DIM ** -0.5 * LOG2E,
            rot_tiles=((3, HEAD_DIM ** -0.5), (4, 1.0)),
            cast=((w_out_att[l], False), (w_out_ret[l], False), (w_out[l], False)))
        xt = _mixer_out(proj, xt, _rel_row(rel_bias[l]), woa, wor, wo, batch, seq)
        xt, _, _ = _ffn(xt, norm_ffn2_g[l][None], wg2, wu2, wd2, norm_final_g[None] if last else None,
                        tm=FFN_TM, tf=FFN_TF)
    return xt.reshape(batch, seq, d)
```

```python
import functools
import math

import numpy as np
import jax
import jax.numpy as jnp
from jax import lax
from jax.experimental import pallas as pl
from jax.experimental.pallas import tpu as pltpu

F32 = jnp.float32
BF16 = jnp.bfloat16

CHUNK = 64
CHUNK_SHIFT = CHUNK.bit_length() - 1
N_PREV_CHUNKS = 8
HEADS = 8
HEAD_DIM = 128
WIDTH = HEADS * HEAD_DIM
MAX_REL_DIST = 128
ROPE_BASE = 10000.0
EPS = 1e-6
NEG = -1e30
LOG2E = math.log2(math.e)

LANES = 128
BF16_SUBLANES = 16

TQ = 256
N_KBLK = 3
FFN_TM = 1024
FFN_TF = 512
FFN_PARTS = 2
MXU_WIDTH = 256
OUT_PIECE = MXU_WIDTH
MIB = 1024 * 1024
VMEM_LIMIT = 60 * MIB


def _rmsnorm(x, g):
    ms = jnp.mean(x * x, axis=-1, keepdims=True)
    return x * lax.rsqrt(ms + EPS) * g


def _silu(x):
    return x * jax.nn.sigmoid(x)


def _cast_plan(shape, n_i, n_j, blocked):
    r, c = shape
    same = lambda spec, every: (spec, spec, shape, every)
    if blocked:
        assert r % (n_i * BF16_SUBLANES) == 0 and c % (n_j * LANES) == 0, shape
        return (pl.BlockSpec((r // n_i, c // n_j), lambda i, j: (i, j)),
                pl.BlockSpec((None, r // n_i, c // n_j), lambda i, j: (j, i, 0)),
                (n_j, r, c // n_j), True)
    if r % (n_i * n_j * BF16_SUBLANES) == 0:
        return same(pl.BlockSpec((r // (n_i * n_j), c), lambda i, j: (i * n_j + j, 0)), True)
    if r % (n_i * BF16_SUBLANES) == 0 and c % (n_j * LANES) == 0:
        return same(pl.BlockSpec((r // n_i, c // n_j), lambda i, j: (i, j)), True)
    if r % (n_j * BF16_SUBLANES) == 0 and c % (n_i * LANES) == 0:
        return same(pl.BlockSpec((r // n_j, c // n_i), lambda i, j: (j, i)), True)
    if r % (n_i * BF16_SUBLANES) == 0:
        return same(pl.BlockSpec((r // n_i, c), lambda i, j: (i, 0)), False)
    rows = next(k for k in range(BF16_SUBLANES, r + 1, BF16_SUBLANES)
                if r % k == 0 and r // k <= n_i * n_j)
    last = r // rows - 1
    return same(pl.BlockSpec((rows, c), lambda i, j: (jnp.minimum(i * n_j + j, last), 0)), True)


def _cast_side_args(cast, n_i, n_j):
    plans = [_cast_plan(w.shape, n_i, n_j, blocked) for w, blocked in cast]
    order = ([k for k, p in enumerate(plans) if p[3]] + [k for k, p in enumerate(plans) if not p[3]])
    in_specs = [plans[k][0] for k in order]
    out_specs = [plans[k][1] for k in order]
    out_shapes = [jax.ShapeDtypeStruct(plans[k][2], BF16) for k in order]
    n_every = sum(1 for p in plans if p[3])
    return order, in_specs, out_specs, out_shapes, n_every


def _unpermute(values, order):
    out = [None] * len(order)
    for pos, k in enumerate(order):
        out[k] = values[pos]
    return out


def _cast_blocks(pairs):
    for src_ref, dst_ref in pairs:
        dst_ref[...] = src_ref[...].astype(BF16)


def _ffn_kernel(*refs, n_f, final_norm, own_cast, has_lead, n_cast, n_cast_every, n_parts):
    n_in = 6 if final_norm else 5
    n_own = 3 if own_cast else 0
    x_ref, g_ref = refs[:2]
    w_refs = refs[2:5]
    gf_ref = refs[5] if final_norm else None
    lead_ref = refs[n_in + n_cast] if has_lead else None
    n_inputs = n_in + n_cast + (1 if has_lead else 0)
    o_ref = refs[n_inputs]
    own_refs = refs[n_inputs + 1:n_inputs + 1 + n_own]
    casts = list(zip(refs[n_in:n_in + n_cast],
                     refs[n_inputs + 1 + n_own:n_inputs + 1 + n_own + n_cast]))
    cast_every, cast_first = casts[:n_cast_every], casts[n_cast_every:]
    h_ref = refs[n_inputs + 1 + n_own + n_cast]
    i = pl.program_id(0)
    j = pl.program_id(1)

    def partial_down(h, n_parts):
        if own_cast:
            _cast_blocks(zip(w_refs, own_refs))
        wg_ref, wu_ref, wd_ref = own_refs if own_cast else w_refs
        width = wd_ref.shape[0] // n_parts
        acts = []
        for p in range(n_parts):
            c = slice(p * width, (p + 1) * width)
            gate = jnp.dot(h, wg_ref[:, c], preferred_element_type=F32)
            up = jnp.dot(h, wu_ref[:, c], preferred_element_type=F32)
            acts.append((_silu(gate) * up).astype(BF16))
        downs = [jnp.dot(act, wd_ref[p * width:(p + 1) * width, :], preferred_element_type=F32)
                 for p, act in enumerate(acts)]
        return functools.reduce(jnp.add, downs)

    def when(cond):
        return pl.when(cond & (i > 0)) if has_lead else pl.when(cond)

    @when(j == 0)
    def _():
        _cast_blocks(cast_every + cast_first)
        h = _rmsnorm(x_ref[...], g_ref[...]).astype(BF16)
        h_ref[...] = h
        o_ref[...] = partial_down(h, n_parts)

    @when((j > 0) & (j < n_f - 1))
    def _():
        _cast_blocks(cast_every)
        o_ref[...] += partial_down(h_ref[...], n_parts)

    @when(j == n_f - 1)
    def _():
        _cast_blocks(cast_every)
        y = x_ref[...] + 0.5 * (o_ref[...] + partial_down(h_ref[...], 1))
        if final_norm:
            y = _rmsnorm(y, gf_ref[...])
        o_ref[...] = y

    if has_lead:
        @pl.when((i == 0) & (j == 0))
        def _():
            _cast_blocks(cast_every + cast_first)
            copy = pltpu.make_async_copy(lead_ref, o_ref, refs[-1])
            copy.start()
            copy.wait()

        @pl.when((i == 0) & (j > 0))
        def _():
            _cast_blocks(cast_every)


def _ffn(x, g, wg, wu, wd, g_final=None, *, tm, tf, cast=(), n_tiles=None, lead=None):
    t, d = x.shape
    n_i = t // tm if n_tiles is None else n_tiles
    n_f = wd.shape[0] // tf
    assert n_f >= 2
    final_norm = g_final is not None
    own_cast = wg.dtype == F32
    has_lead = lead is not None
    assert not (own_cast and has_lead)

    def fstep(i, j):
        return jnp.where(i == 0, 0, j) if has_lead else j

    def col_tile(w):
        if w.ndim == 3:
            return pl.BlockSpec((None, d, tf), lambda i, j: (fstep(i, j), 0, 0))
        return pl.BlockSpec((d, tf), lambda i, j: (0, fstep(i, j)))

    w_specs = [col_tile(wg), col_tile(wu), pl.BlockSpec((tf, d), lambda i, j: (fstep(i, j), 0))]
    row_spec = pl.BlockSpec((tm, d), lambda i, j: (i, 0))
    in_specs = [row_spec, pl.BlockSpec((1, d), lambda i, j: (0, 0))] + w_specs
    args = [x, g, wg, wu, wd]
    if final_norm:
        in_specs.append(pl.BlockSpec((1, d), lambda i, j: (0, 0)))
        args.append(g_final)
    order, cast_in, cast_out, cast_shapes, n_every = _cast_side_args(cast, n_i, n_f)
    in_specs += cast_in
    args += [cast[k][0] for k in order]
    scratch = [pltpu.VMEM((tm, d), BF16)]
    if has_lead:
        assert lead.shape == (tm, d)
        in_specs.append(pl.BlockSpec(memory_space=pl.ANY))
        args.append(lead)
        scratch.append(pltpu.SemaphoreType.DMA(()))
    own_specs = w_specs if own_cast else []
    own_shapes = [jax.ShapeDtypeStruct(w.shape, BF16) for w in (wg, wu, wd)] if own_cast else []
    outs = pl.pallas_call(
        functools.partial(_ffn_kernel, n_f=n_f, final_norm=final_norm, own_cast=own_cast,
                          has_lead=has_lead, n_cast=len(cast), n_cast_every=n_every,
                          n_parts=FFN_PARTS if tf % (FFN_PARTS * MXU_WIDTH) == 0 else 1),
        grid=(n_i, n_f),
        in_specs=in_specs,
        out_specs=[row_spec] + own_specs + cast_out,
        out_shape=[jax.ShapeDtypeStruct((n_i * tm, d), F32)] + own_shapes + cast_shapes,
        scratch_shapes=scratch,
        compiler_params=pltpu.CompilerParams(
            dimension_semantics=("arbitrary", "arbitrary"),
            vmem_limit_bytes=VMEM_LIMIT),
        name="ffn_final" if final_norm else ("ffn_lead" if own_cast else "ffn"),
    )(*args)
    n_own = len(own_shapes)
    return outs[0], _unpermute(outs[1 + n_own:], order), list(outs[1:1 + n_own])


def _in_proj_kernel(*refs, first_tile_scale, rot_tiles, n_cast, n_cast_every):
    x_ref, g_ref, w_ref, cos_ref, sin_ref = refs[:5]
    o_ref = refs[5 + n_cast]
    h_ref = refs[-1]
    casts = list(zip(refs[5:5 + n_cast], refs[6 + n_cast:6 + 2 * n_cast]))
    cast_every, cast_first = casts[:n_cast_every], casts[n_cast_every:]
    j = pl.program_id(1)

    def project(h):
        return jnp.dot(h, w_ref[...], preferred_element_type=F32)

    @pl.when(j == 0)
    def _():
        _cast_blocks(cast_every + cast_first)
        h = _rmsnorm(x_ref[...], g_ref[...]).astype(BF16)
        h_ref[...] = h
        o_ref[...] = (project(h) * first_tile_scale).astype(BF16)

    plain = j > 0
    for tile, scale in rot_tiles:
        plain = plain & (j != tile)

        @pl.when(j == tile)
        def _(scale=scale):
            _cast_blocks(cast_every)
            acc = project(h_ref[...])
            cos = cos_ref[...]
            sin = sin_ref[...]
            for hd in range(o_ref.shape[1] // HEAD_DIM):
                cols = slice(hd * HEAD_DIM, (hd + 1) * HEAD_DIM)
                xh = acc[:, cols]
                rot = xh * cos + pltpu.roll(xh, HEAD_DIM // 2, axis=1) * sin
                o_ref[:, cols] = (rot * scale).astype(BF16)

    @pl.when(plain)
    def _():
        _cast_blocks(cast_every)
        o_ref[...] = project(h_ref[...]).astype(BF16)


def _in_proj(x, g, w, cos2, sin2, *, first_tile_scale, rot_tiles, cast=(), tm=1024, tn=WIDTH):
    t, d = x.shape
    n_j = w.shape[0]
    assert w.shape[1:] == (d, tn)
    n_i = t // tm
    seq_tiles = cos2.shape[0] // tm
    order, cast_in, cast_out, cast_shapes, n_every = _cast_side_args(cast, n_i, n_j)
    table = pl.BlockSpec((tm, HEAD_DIM), lambda i, j: (lax.rem(i, seq_tiles), 0))
    outs = pl.pallas_call(
        functools.partial(_in_proj_kernel, first_tile_scale=first_tile_scale, rot_tiles=rot_tiles,
                          n_cast=len(cast), n_cast_every=n_every),
        grid=(n_i, n_j),
        in_specs=[
            pl.BlockSpec((tm, d), lambda i, j: (i, 0)),
            pl.BlockSpec((1, d), lambda i, j: (0, 0)),
            pl.BlockSpec((None, d, tn), lambda i, j: (j, 0, 0)),
            table, table,
        ] + cast_in,
        out_specs=[pl.BlockSpec((None, tm, tn), lambda i, j: (j, i, 0))] + cast_out,
        out_shape=[jax.ShapeDtypeStruct((n_j, t, tn), BF16)] + cast_shapes,
        scratch_shapes=[pltpu.VMEM((tm, d), BF16)],
        compiler_params=pltpu.CompilerParams(
            dimension_semantics=("arbitrary", "arbitrary"),
            vmem_limit_bytes=VMEM_LIMIT),
        name="in_proj",
    )(x, g, w, cos2, sin2, *[cast[k][0] for k in order])
    return outs[0], _unpermute(outs[1:], order)


_LOG_GAMMA = tuple(math.log(1.0 - 2.0 ** (-5.0 - h)) for h in range(HEADS))


def _dot_nt(a, b):
    return lax.dot_general(a, b, (((1,), (1,)), ((), ())), preferred_element_type=F32)


def _lane_slabs(x):
    return [x[:, k:k + LANES] for k in range(0, x.shape[1], LANES)]


def _dot_tn(a, b):
    return lax.dot_general(a, b, (((0,), (0,)), ((), ())), preferred_element_type=F32)


def _mixer_out_kernel(qa_ref, ka0_ref, ka1_ref, ka2_ref, va0_ref, va1_ref, va2_ref,
                      qr_ref, kr_ref, vr_ref, gr_ref, ga0_ref, ga1_ref, gm0_ref, gm1_ref,
                      relrow_ref, xprev_ref, woa_ref, wor_ref, wo_ref,
                      o_ref,
                      bias_ref, decay_ref, state_ref,
                      att_ref, ret_ref, attp_ref, retp_ref, merged_ref, *, n_tiles, tiles_per_seq):
    s = pl.program_id(0)
    t = jnp.minimum(s, n_tiles - 1)
    i = lax.rem(t, tiles_per_seq)
    ka_refs = (ka0_ref, ka1_ref, ka2_ref)
    va_refs = (va0_ref, va1_ref, va2_ref)

    @pl.when(s == 0)
    def _():
        qc = lax.broadcasted_iota(jnp.int32, (TQ, N_KBLK * TQ), 0) >> CHUNK_SHIFT
        kc = lax.broadcasted_iota(jnp.int32, (TQ, N_KBLK * TQ), 1) >> CHUNK_SHIFT
        in_band = (kc >= qc) & (kc <= qc + N_PREV_CHUNKS)
        for h in range(HEADS):
            rows = jnp.broadcast_to(relrow_ref[h:h + 1, :], (TQ, (N_KBLK + 1) * TQ))
            toeplitz = pltpu.roll(rows, 0, axis=1, stride=1, stride_axis=0)
            bias_ref[h] = jnp.where(in_band, toeplitz[:, TQ:] * LOG2E, NEG)
        bias_ref[HEADS] = jnp.full((TQ, N_KBLK * TQ), NEG, F32)

        n = lax.broadcasted_iota(jnp.int32, (TQ, TQ), 0)
        m = lax.broadcasted_iota(jnp.int32, (TQ, TQ), 1)
        dist = jnp.abs(n - m).astype(F32)
        visible = (m >> CHUNK_SHIFT) <= (n >> CHUNK_SHIFT)
        for h in range(HEADS):
            decay_ref[h] = jnp.where(visible, jnp.exp(_LOG_GAMMA[h] * dist), 0.0)
        att_ref[...] = jnp.zeros_like(att_ref)
        ret_ref[...] = jnp.zeros_like(ret_ref)

    @pl.when(i == 0)
    def _():
        state_ref[...] = jnp.zeros_like(state_ref)

    attp_ref[...] = att_ref[...]
    retp_ref[...] = ret_ref[...]

    row = lax.broadcasted_iota(jnp.int32, (TQ, HEAD_DIM), 0).astype(F32)

    def scores(h):
        cols = slice(h * HEAD_DIM, (h + 1) * HEAD_DIM)
        q = qa_ref[:, cols]
        sc = []
        for j in range(N_KBLK):
            plane = h if j == N_KBLK - 1 else jnp.where(i + j >= N_KBLK - 1, h, HEADS)
            sc.append(_dot_nt(q, ka_refs[j][:, cols]) + bias_ref[plane, :, j * TQ:(j + 1) * TQ])
        return sc

    def attend(h, sc):
        cols = slice(h * HEAD_DIM, (h + 1) * HEAD_DIM)
        mx = jnp.maximum(jnp.maximum(jnp.max(sc[0], axis=-1, keepdims=True),
                                     jnp.max(sc[1], axis=-1, keepdims=True)),
                         jnp.max(sc[2], axis=-1, keepdims=True))
        acc = None
        den = None
        for j in range(N_KBLK):
            p = jnp.exp2(sc[j] - mx)
            dj = jnp.sum(p, axis=-1, keepdims=True)
            oj = jnp.dot(p.astype(BF16), va_refs[j][:, cols], preferred_element_type=F32)
            acc = oj if acc is None else acc + oj
            den = dj if den is None else den + dj
        att_ref[:, cols] = (acc / den).astype(BF16)

    def retain(h, fill):
        cols = slice(h * HEAD_DIM, (h + 1) * HEAD_DIM)
        qr = qr_ref[:, cols]
        kr = kr_ref[:, cols]
        kr_f = kr.astype(F32)
        v = vr_ref[:, cols]
        sr = _dot_nt(qr, kr)
        st = state_ref[h]
        lg = _LOG_GAMMA[h]
        cross = jnp.dot(qr, st.astype(BF16), preferred_element_type=F32)
        fill()
        inner = jnp.dot((sr * decay_ref[h]).astype(BF16), v, preferred_element_type=F32)
        kd = (kr_f * jnp.exp(lg * (TQ - 1.0 - row))).astype(BF16)
        state_ref[h] = st * math.exp(lg * TQ) + _dot_tn(kd, v)
        out = inner + cross * jnp.exp(lg * (row + 1.0))
        out = out * lax.rsqrt(jnp.mean(out * out, axis=-1, keepdims=True) + EPS)
        ret_ref[:, cols] = (_silu(gr_ref[:, cols].astype(F32)) * out).astype(BF16)

    gate_refs = ((ga0_ref, gm0_ref), (ga1_ref, gm1_ref))

    def merge_piece(k):
        cols = slice(k * OUT_PIECE, (k + 1) * OUT_PIECE)
        ga_ref, gm_ref = gate_refs[k * OUT_PIECE // WIDTH]
        gcols = slice(k * OUT_PIECE % WIDTH, k * OUT_PIECE % WIDTH + OUT_PIECE)
        branch_a = jnp.dot(attp_ref[...], woa_ref[:, cols], preferred_element_type=F32)
        branch_r = jnp.dot(retp_ref[...], wor_ref[:, cols], preferred_element_type=F32)
        merged = (jax.nn.sigmoid(ga_ref[:, gcols].astype(F32)) * branch_a
                  + jax.nn.sigmoid(gm_ref[:, gcols].astype(F32)) * branch_r)
        merged_ref[:, cols] = merged.astype(BF16)

    def out_piece(k):
        cols = slice(k * OUT_PIECE, (k + 1) * OUT_PIECE)
        o_ref[:, cols] = xprev_ref[:, cols] + jnp.dot(merged_ref[...], wo_ref[:, cols],
                                                      preferred_element_type=F32)

    n_pieces = merged_ref.shape[1] // OUT_PIECE
    pieces = ([functools.partial(merge_piece, k) for k in range(n_pieces)]
              + [functools.partial(out_piece, k) for k in range(n_pieces)])
    assert len(pieces) == 2 * HEADS
    sc = scores(0)
    for h in range(HEADS):
        sc_next = scores(h + 1) if h + 1 < HEADS else None
        pieces[2 * h]()
        attend(h, sc)
        retain(h, pieces[2 * h + 1])
        sc = sc_next


def _mixer_out(proj, x, rel_row, woa, wor, wo, batch, seq):
    t, d = x.shape
    tiles_per_seq = seq // TQ
    n_tiles = batch * tiles_per_seq
    proj4 = proj.reshape(proj.shape[0], batch, seq, WIDTH)

    def split(tt):
        return tt // tiles_per_seq, lax.rem(tt, tiles_per_seq)

    def cur(s):
        return split(jnp.minimum(s, n_tiles - 1))

    def tile(col):
        return pl.BlockSpec((None, None, TQ, WIDTH), lambda s: (col, *cur(s), 0))

    def prev_tile(col):
        return pl.BlockSpec((None, None, TQ, WIDTH),
                            lambda s: (col, *split(jnp.maximum(s - 1, 0)), 0))

    def hist(col, j):
        def index(s):
            b, i = cur(s)
            return col, b, jnp.maximum(i - (N_KBLK - 1) + j, 0), 0
        return pl.BlockSpec((None, None, TQ, WIDTH), index)

    def resident(shape):
        return pl.BlockSpec(shape, lambda s: (0, 0), pipeline_mode=pl.Buffered(1))

    prev = pl.BlockSpec((TQ, d), lambda s: (jnp.maximum(s - 1, 0), 0))
    gate0 = 7
    in_specs = ([tile(0)] + [hist(1, j) for j in range(N_KBLK)] + [hist(2, j) for j in range(N_KBLK)]
                + [tile(3), tile(4), tile(5), tile(6)]
                + [prev_tile(gate0 + c) for c in range(4)]
                + [resident(rel_row.shape), prev,
                   resident(woa.shape), resident(wor.shape), resident(wo.shape)])
    return pl.pallas_call(
        functools.partial(_mixer_out_kernel, n_tiles=n_tiles, tiles_per_seq=tiles_per_seq),
        grid=(n_tiles + 1,),
        in_specs=in_specs,
        out_specs=prev,
        out_shape=jax.ShapeDtypeStruct((t, d), F32),
        scratch_shapes=[pltpu.VMEM((HEADS + 1, TQ, N_KBLK * TQ), F32),
                        pltpu.VMEM((HEADS, TQ, TQ), F32),
                        pltpu.VMEM((HEADS, HEAD_DIM, HEAD_DIM), F32),
                        pltpu.VMEM((TQ, WIDTH), BF16),
                        pltpu.VMEM((TQ, WIDTH), BF16),
                        pltpu.VMEM((TQ, WIDTH), BF16),
                        pltpu.VMEM((TQ, WIDTH), BF16),
                        pltpu.VMEM((TQ, d), BF16)],
        compiler_params=pltpu.CompilerParams(
            dimension_semantics=("arbitrary",),
            vmem_limit_bytes=VMEM_LIMIT),
        name="mixer_out",
    )(*([proj4] * 15), rel_row, x, woa, wor, wo)


def _rotary_tables(seq):
    inv = 1.0 / (ROPE_BASE ** (np.arange(0, HEAD_DIM, 2, dtype=np.float64) / HEAD_DIM))
    ang = np.arange(seq, dtype=np.float64)[:, None] * inv[None, :]
    cos, sin = np.cos(ang), np.sin(ang)
    return (np.concatenate([cos, cos], axis=-1).astype(np.float32),
            np.concatenate([-sin, sin], axis=-1).astype(np.float32))


def _rel_row(rel_bias):
    j = np.arange((N_KBLK + 1) * TQ)
    rel_idx = np.clip(N_KBLK * TQ - j, -MAX_REL_DIST, MAX_REL_DIST) + MAX_REL_DIST
    return rel_bias.astype(F32)[:, rel_idx]


def kernel(x, norm_ffn1_g, ffn1_w_gate, ffn1_w_up, ffn1_w_down, norm_mix_g, w_in, rel_bias,
           w_out_att, w_out_ret, w_out, norm_ffn2_g, ffn2_w_gate, ffn2_w_up, ffn2_w_down,
           norm_final_g):
    batch, seq, d = x.shape
    depth = w_in.shape[0]
    assert depth >= 1 and (N_KBLK - 1) * TQ == N_PREV_CHUNKS * CHUNK and seq % TQ == 0
    cos2, sin2 = _rotary_tables(seq)
    xt = x.reshape(batch * seq, d)
    for l in range(depth):
        last = l == depth - 1
        g1 = norm_ffn1_g[l][None]
        y_lead, _, (wg1, wu1, wd1) = _ffn(xt, g1, ffn1_w_gate[l], ffn1_w_up[l], ffn1_w_down[l],
                                          n_tiles=1, tm=FFN_TM, tf=FFN_TF // 2)
        xt, (w_in_b, wg2, wu2, wd2), _ = _ffn(
            xt, g1, wg1, wu1, wd1, lead=y_lead, tm=FFN_TM, tf=FFN_TF,
            cast=((w_in[l], True), (ffn2_w_gate[l], True), (ffn2_w_up[l], True),
                  (ffn2_w_down[l], False)))
        proj, (woa, wor, wo) = _in_proj(
            xt, norm_mix_g[l][None], w_in_b, cos2, sin2, first_tile_scale=HEAD_DIM ** -0.5 * LOG2E,
            rot_tiles=((3, HEAD_DIM ** -0.5), (4, 1.0)),
            cast=((w_out_att[l], False), (w_out_ret[l], False), (w_out[l], False)))
        xt = _mixer_out(proj, xt, _rel_row(rel_bias[l]), woa, wor, wo, batch, seq)
        xt, _, _ = _ffn(xt, norm_ffn2_g[l][None], wg2, wu2, wd2, norm_final_g[None] if last else None,
                        tm=FFN_TM, tf=FFN_TF)
    return xt.reshape(batch, seq, d)
```

```python
import functools
import math

import numpy as np
import jax
import jax.numpy as jnp
from jax import lax
from jax.experimental import pallas as pl
from jax.experimental.pallas import tpu as pltpu

F32 = jnp.float32
BF16 = jnp.bfloat16

CHUNK = 64
CHUNK_SHIFT = CHUNK.bit_length() - 1
N_PREV_CHUNKS = 8
HEADS = 8
HEAD_DIM = 128
WIDTH = HEADS * HEAD_DIM
MAX_REL_DIST = 128
ROPE_BASE = 10000.0
EPS = 1e-6
NEG = -1e30
LOG2E = math.log2(math.e)

LANES = 128
BF16_SUBLANES = 16

TQ = 256
N_KBLK = 3
FFN_TM = 1024
FFN_TF = 512
FFN_PARTS = 2
W_RING = 3
MXU_WIDTH = 256
OUT_PIECE = MXU_WIDTH
MIB = 1024 * 1024
VMEM_LIMIT = 60 * MIB


def _rmsnorm(x, g):
    ms = jnp.mean(x * x, axis=-1, keepdims=True)
    return x * lax.rsqrt(ms + EPS) * g


def _silu(x):
    return x * jax.nn.sigmoid(x)


def _cast_plan(shape, n_i, n_j, blocked):
    r, c = shape
    same = lambda spec, every: (spec, spec, shape, every)
    if blocked:
        assert r % (n_i * BF16_SUBLANES) == 0 and c % (n_j * LANES) == 0, shape
        return (pl.BlockSpec((r // n_i, c // n_j), lambda i, j: (i, j)),
                pl.BlockSpec((None, r // n_i, c // n_j), lambda i, j: (j, i, 0)),
                (n_j, r, c // n_j), True)
    if r % (n_i * n_j * BF16_SUBLANES) == 0:
        return same(pl.BlockSpec((r // (n_i * n_j), c), lambda i, j: (i * n_j + j, 0)), True)
    if r % (n_i * BF16_SUBLANES) == 0 and c % (n_j * LANES) == 0:
        return same(pl.BlockSpec((r // n_i, c // n_j), lambda i, j: (i, j)), True)
    if r % (n_j * BF16_SUBLANES) == 0 and c % (n_i * LANES) == 0:
        return same(pl.BlockSpec((r // n_j, c // n_i), lambda i, j: (j, i)), True)
    if r % (n_i * BF16_SUBLANES) == 0:
        return same(pl.BlockSpec((r // n_i, c), lambda i, j: (i, 0)), False)
    rows = next(k for k in range(BF16_SUBLANES, r + 1, BF16_SUBLANES)
                if r % k == 0 and r // k <= n_i * n_j)
    last = r // rows - 1
    return same(pl.BlockSpec((rows, c), lambda i, j: (jnp.minimum(i * n_j + j, last), 0)), True)


def _cast_side_args(cast, n_i, n_j):
    plans = [_cast_plan(w.shape, n_i, n_j, blocked) for w, blocked in cast]
    order = ([k for k, p in enumerate(plans) if p[3]] + [k for k, p in enumerate(plans) if not p[3]])
    in_specs = [plans[k][0] for k in order]
    out_specs = [plans[k][1] for k in order]
    out_shapes = [jax.ShapeDtypeStruct(plans[k][2], BF16) for k in order]
    n_every = sum(1 for p in plans if p[3])
    return order, in_specs, out_specs, out_shapes, n_every


def _unpermute(values, order):
    out = [None] * len(order)
    for pos, k in enumerate(order):
        out[k] = values[pos]
    return out


def _cast_blocks(pairs):
    for src_ref, dst_ref in pairs:
        dst_ref[...] = src_ref[...].astype(BF16)


def _ffn_kernel(*refs, n_f, final_norm, own_cast, has_lead, n_cast, n_cast_every, n_parts):
    n_in = 6 if final_norm else 5
    n_own = 3 if own_cast else 0
    x_ref, g_ref = refs[:2]
    w_refs = refs[2:5]
    gf_ref = refs[5] if final_norm else None
    lead_ref = refs[n_in + n_cast] if has_lead else None
    n_inputs = n_in + n_cast + (1 if has_lead else 0)
    o_ref = refs[n_inputs]
    own_refs = refs[n_inputs + 1:n_inputs + 1 + n_own]
    casts = list(zip(refs[n_in:n_in + n_cast],
                     refs[n_inputs + 1 + n_own:n_inputs + 1 + n_own + n_cast]))
    cast_every, cast_first = casts[:n_cast_every], casts[n_cast_every:]
    h_ref = refs[n_inputs + 1 + n_own + n_cast]
    i = pl.program_id(0)
    j = pl.program_id(1)

    def partial_down(h, n_parts):
        if own_cast:
            _cast_blocks(zip(w_refs, own_refs))
        wg_ref, wu_ref, wd_ref = own_refs if own_cast else w_refs
        width = wd_ref.shape[0] // n_parts
        acts = []
        for p in range(n_parts):
            c = slice(p * width, (p + 1) * width)
            gate = jnp.dot(h, wg_ref[:, c], preferred_element_type=F32)
            up = jnp.dot(h, wu_ref[:, c], preferred_element_type=F32)
            acts.append((_silu(gate) * up).astype(BF16))
        downs = [jnp.dot(act, wd_ref[p * width:(p + 1) * width, :], preferred_element_type=F32)
                 for p, act in enumerate(acts)]
        return functools.reduce(jnp.add, downs)

    def when(cond):
        return pl.when(cond & (i > 0)) if has_lead else pl.when(cond)

    @when(j == 0)
    def _():
        _cast_blocks(cast_every + cast_first)
        h = _rmsnorm(x_ref[...], g_ref[...]).astype(BF16)
        h_ref[...] = h
        o_ref[...] = partial_down(h, n_parts)

    @when((j > 0) & (j < n_f - 1))
    def _():
        _cast_blocks(cast_every)
        o_ref[...] += partial_down(h_ref[...], n_parts)

    @when(j == n_f - 1)
    def _():
        _cast_blocks(cast_every)
        y = x_ref[...] + 0.5 * (o_ref[...] + partial_down(h_ref[...], 1))
        if final_norm:
            y = _rmsnorm(y, gf_ref[...])
        o_ref[...] = y

    if has_lead:
        @pl.when((i == 0) & (j == 0))
        def _():
            _cast_blocks(cast_every + cast_first)
            copy = pltpu.make_async_copy(lead_ref, o_ref, refs[-1])
            copy.start()
            copy.wait()

        @pl.when((i == 0) & (j > 0))
        def _():
            _cast_blocks(cast_every)


def _ffn(x, g, wg, wu, wd, g_final=None, *, tm, tf, cast=(), n_tiles=None, lead=None):
    t, d = x.shape
    n_i = t // tm if n_tiles is None else n_tiles
    n_f = wd.shape[0] // tf
    assert n_f >= 2
    final_norm = g_final is not None
    own_cast = wg.dtype == F32
    has_lead = lead is not None
    assert not (own_cast and has_lead)

    def fstep(i, j):
        return jnp.where(i == 0, 0, j) if has_lead else j

    def col_tile(w):
        if w.ndim == 3:
            return pl.BlockSpec((None, d, tf), lambda i, j: (fstep(i, j), 0, 0))
        return pl.BlockSpec((d, tf), lambda i, j: (0, fstep(i, j)))

    w_specs = [col_tile(wg), col_tile(wu), pl.BlockSpec((tf, d), lambda i, j: (fstep(i, j), 0))]
    row_spec = pl.BlockSpec((tm, d), lambda i, j: (i, 0))
    in_specs = [row_spec, pl.BlockSpec((1, d), lambda i, j: (0, 0))] + w_specs
    args = [x, g, wg, wu, wd]
    if final_norm:
        in_specs.append(pl.BlockSpec((1, d), lambda i, j: (0, 0)))
        args.append(g_final)
    order, cast_in, cast_out, cast_shapes, n_every = _cast_side_args(cast, n_i, n_f)
    in_specs += cast_in
    args += [cast[k][0] for k in order]
    scratch = [pltpu.VMEM((tm, d), BF16)]
    if has_lead:
        assert lead.shape == (tm, d)
        in_specs.append(pl.BlockSpec(memory_space=pl.ANY))
        args.append(lead)
        scratch.append(pltpu.SemaphoreType.DMA(()))
    own_specs = w_specs if own_cast else []
    own_shapes = [jax.ShapeDtypeStruct(w.shape, BF16) for w in (wg, wu, wd)] if own_cast else []
    outs = pl.pallas_call(
        functools.partial(_ffn_kernel, n_f=n_f, final_norm=final_norm, own_cast=own_cast,
                          has_lead=has_lead, n_cast=len(cast), n_cast_every=n_every,
                          n_parts=FFN_PARTS if tf % (FFN_PARTS * MXU_WIDTH) == 0 else 1),
        grid=(n_i, n_f),
        in_specs=in_specs,
        out_specs=[row_spec] + own_specs + cast_out,
        out_shape=[jax.ShapeDtypeStruct((n_i * tm, d), F32)] + own_shapes + cast_shapes,
        scratch_shapes=scratch,
        compiler_params=pltpu.CompilerParams(
            dimension_semantics=("arbitrary", "arbitrary"),
            vmem_limit_bytes=VMEM_LIMIT),
        name="ffn_final" if final_norm else ("ffn_lead" if own_cast else "ffn"),
    )(*args)
    n_own = len(own_shapes)
    return outs[0], _unpermute(outs[1 + n_own:], order), list(outs[1:1 + n_own])


def _in_proj_kernel(*refs, first_tile_scale, rot_tiles, n_cast, n_cast_every, n_i, n_j):
    x_ref, g_ref, w_hbm_ref, cos_ref, sin_ref = refs[:5]
    o_ref = refs[5 + n_cast]
    h_ref, wbuf_ref, wsem_ref = refs[-3:]
    casts = list(zip(refs[5:5 + n_cast], refs[6 + n_cast:6 + 2 * n_cast]))
    cast_every, cast_first = casts[:n_cast_every], casts[n_cast_every:]
    j = pl.program_id(1)
    step = pl.program_id(0) * n_j + j
    n_steps = n_i * n_j

    def rem(a, b):
        return a % b if isinstance(a, int) else lax.rem(a, b)

    def tile_copy(s):
        slot = rem(s, W_RING)
        return pltpu.make_async_copy(w_hbm_ref.at[rem(s, n_j)], wbuf_ref.at[slot], wsem_ref.at[slot])

    @pl.when(step == 0)
    def _():
        for s in range(W_RING - 1):
            tile_copy(s).start()

    @pl.when(step + (W_RING - 1) < n_steps)
    def _():
        tile_copy(step + (W_RING - 1)).start()

    tile_copy(step).wait()
    w_ref = wbuf_ref.at[rem(step, W_RING)]

    def project(h):
        return jnp.dot(h, w_ref[...], preferred_element_type=F32)

    @pl.when(j == 0)
    def _():
        _cast_blocks(cast_every + cast_first)
        h = _rmsnorm(x_ref[...], g_ref[...]).astype(BF16)
        h_ref[...] = h
        o_ref[...] = (project(h) * first_tile_scale).astype(BF16)

    plain = j > 0
    for tile, scale in rot_tiles:
        plain = plain & (j != tile)

        @pl.when(j == tile)
        def _(scale=scale):
            _cast_blocks(cast_every)
            acc = project(h_ref[...])
            cos = cos_ref[...]
            sin = sin_ref[...]
            for hd in range(o_ref.shape[1] // HEAD_DIM):
                cols = slice(hd * HEAD_DIM, (hd + 1) * HEAD_DIM)
                xh = acc[:, cols]
                rot = xh * cos + pltpu.roll(xh, HEAD_DIM // 2, axis=1) * sin
                o_ref[:, cols] = (rot * scale).astype(BF16)

    @pl.when(plain)
    def _():
        _cast_blocks(cast_every)
        o_ref[...] = project(h_ref[...]).astype(BF16)


def _in_proj(x, g, w, cos2, sin2, *, first_tile_scale, rot_tiles, cast=(), tm=1024, tn=WIDTH):
    t, d = x.shape
    n_j = w.shape[0]
    assert w.shape[1:] == (d, tn)
    n_i = t // tm
    seq_tiles = cos2.shape[0] // tm
    order, cast_in, cast_out, cast_shapes, n_every = _cast_side_args(cast, n_i, n_j)
    table = pl.BlockSpec((tm, HEAD_DIM), lambda i, j: (lax.rem(i, seq_tiles), 0))
    outs = pl.pallas_call(
        functools.partial(_in_proj_kernel, first_tile_scale=first_tile_scale, rot_tiles=rot_tiles,
                          n_cast=len(cast), n_cast_every=n_every, n_i=n_i, n_j=n_j),
        grid=(n_i, n_j),
        in_specs=[
            pl.BlockSpec((tm, d), lambda i, j: (i, 0)),
            pl.BlockSpec((1, d), lambda i, j: (0, 0)),
            pl.BlockSpec(memory_space=pl.ANY),
            table, table,
        ] + cast_in,
        out_specs=[pl.BlockSpec((None, tm, tn), lambda i, j: (j, i, 0))] + cast_out,
        out_shape=[jax.ShapeDtypeStruct((n_j, t, tn), BF16)] + cast_shapes,
        scratch_shapes=[pltpu.VMEM((tm, d), BF16), pltpu.VMEM((W_RING, d, tn), BF16),
                        pltpu.SemaphoreType.DMA((W_RING,))],
        compiler_params=pltpu.CompilerParams(
            dimension_semantics=("arbitrary", "arbitrary"),
            vmem_limit_bytes=VMEM_LIMIT),
        name="in_proj",
    )(x, g, w, cos2, sin2, *[cast[k][0] for k in order])
    return outs[0], _unpermute(outs[1:], order)


_LOG_GAMMA = tuple(math.log(1.0 - 2.0 ** (-5.0 - h)) for h in range(HEADS))


def _dot_nt(a, b):
    return lax.dot_general(a, b, (((1,), (1,)), ((), ())), preferred_element_type=F32)


def _lane_slabs(x):
    return [x[:, k:k + LANES] for k in range(0, x.shape[1], LANES)]


def _dot_tn(a, b):
    return lax.dot_general(a, b, (((0,), (0,)), ((), ())), preferred_element_type=F32)


def _mixer_out_kernel(qa_ref, ka0_ref, ka1_ref, ka2_ref, va0_ref, va1_ref, va2_ref,
                      qr_ref, kr_ref, vr_ref, gr_ref, ga0_ref, ga1_ref, gm0_ref, gm1_ref,
                      relrow_ref, xprev_ref, woa_ref, wor_ref, wo_ref,
                      o_ref,
                      bias_ref, decay_ref, state_ref,
                      att_ref, ret_ref, attp_ref, retp_ref, merged_ref, *, n_tiles, tiles_per_seq):
    s = pl.program_id(0)
    t = jnp.minimum(s, n_tiles - 1)
    i = lax.rem(t, tiles_per_seq)
    ka_refs = (ka0_ref, ka1_ref, ka2_ref)
    va_refs = (va0_ref, va1_ref, va2_ref)

    @pl.when(s == 0)
    def _():
        qc = lax.broadcasted_iota(jnp.int32, (TQ, N_KBLK * TQ), 0) >> CHUNK_SHIFT
        kc = lax.broadcasted_iota(jnp.int32, (TQ, N_KBLK * TQ), 1) >> CHUNK_SHIFT
        in_band = (kc >= qc) & (kc <= qc + N_PREV_CHUNKS)
        for h in range(HEADS):
            rows = jnp.broadcast_to(relrow_ref[h:h + 1, :], (TQ, (N_KBLK + 1) * TQ))
            toeplitz = pltpu.roll(rows, 0, axis=1, stride=1, stride_axis=0)
            bias_ref[h] = jnp.where(in_band, toeplitz[:, TQ:] * LOG2E, NEG)
        bias_ref[HEADS] = jnp.full((TQ, N_KBLK * TQ), NEG, F32)

        n = lax.broadcasted_iota(jnp.int32, (TQ, TQ), 0)
        m = lax.broadcasted_iota(jnp.int32, (TQ, TQ), 1)
        dist = jnp.abs(n - m).astype(F32)
        visible = (m >> CHUNK_SHIFT) <= (n >> CHUNK_SHIFT)
        for h in range(HEADS):
            decay_ref[h] = jnp.where(visible, jnp.exp(_LOG_GAMMA[h] * dist), 0.0)
        att_ref[...] = jnp.zeros_like(att_ref)
        ret_ref[...] = jnp.zeros_like(ret_ref)

    @pl.when(i == 0)
    def _():
        state_ref[...] = jnp.zeros_like(state_ref)

    attp_ref[...] = att_ref[...]
    retp_ref[...] = ret_ref[...]

    row = lax.broadcasted_iota(jnp.int32, (TQ, HEAD_DIM), 0).astype(F32)

    def scores(h):
        cols = slice(h * HEAD_DIM, (h + 1) * HEAD_DIM)
        q = qa_ref[:, cols]
        sc = []
        for j in range(N_KBLK):
            plane = h if j == N_KBLK - 1 else jnp.where(i + j >= N_KBLK - 1, h, HEADS)
            sc.append(_dot_nt(q, ka_refs[j][:, cols]) + bias_ref[plane, :, j * TQ:(j + 1) * TQ])
        return sc

    def attend(h, sc):
        cols = slice(h * HEAD_DIM, (h + 1) * HEAD_DIM)
        mx = jnp.maximum(jnp.maximum(jnp.max(sc[0], axis=-1, keepdims=True),
                                     jnp.max(sc[1], axis=-1, keepdims=True)),
                         jnp.max(sc[2], axis=-1, keepdims=True))
        acc = None
        den = None
        for j in range(N_KBLK):
            p = jnp.exp2(sc[j] - mx)
            dj = jnp.sum(p, axis=-1, keepdims=True)
            oj = jnp.dot(p.astype(BF16), va_refs[j][:, cols], preferred_element_type=F32)
            acc = oj if acc is None else acc + oj
            den = dj if den is None else den + dj
        att_ref[:, cols] = (acc / den).astype(BF16)

    def retain(h, fill):
        cols = slice(h * HEAD_DIM, (h + 1) * HEAD_DIM)
        qr = qr_ref[:, cols]
        kr = kr_ref[:, cols]
        kr_f = kr.astype(F32)
        v = vr_ref[:, cols]
        sr = _dot_nt(qr, kr)
        st = state_ref[h]
        lg = _LOG_GAMMA[h]
        cross = jnp.dot(qr, st.astype(BF16), preferred_element_type=F32)
        fill()
        inner = jnp.dot((sr * decay_ref[h]).astype(BF16), v, preferred_element_type=F32)
        kd = (kr_f * jnp.exp(lg * (TQ - 1.0 - row))).astype(BF16)
        state_ref[h] = st * math.exp(lg * TQ) + _dot_tn(kd, v)
        out = inner + cross * jnp.exp(lg * (row + 1.0))
        out = out * lax.rsqrt(jnp.mean(out * out, axis=-1, keepdims=True) + EPS)
        ret_ref[:, cols] = (_silu(gr_ref[:, cols].astype(F32)) * out).astype(BF16)

    gate_refs = ((ga0_ref, gm0_ref), (ga1_ref, gm1_ref))

    def merge_piece(k):
        cols = slice(k * OUT_PIECE, (k + 1) * OUT_PIECE)
        ga_ref, gm_ref = gate_refs[k * OUT_PIECE // WIDTH]
        gcols = slice(k * OUT_PIECE % WIDTH, k * OUT_PIECE % WIDTH + OUT_PIECE)
        branch_a = jnp.dot(attp_ref[...], woa_ref[:, cols], preferred_element_type=F32)
        branch_r = jnp.dot(retp_ref[...], wor_ref[:, cols], preferred_element_type=F32)
        merged = (jax.nn.sigmoid(ga_ref[:, gcols].astype(F32)) * branch_a
                  + jax.nn.sigmoid(gm_ref[:, gcols].astype(F32)) * branch_r)
        merged_ref[:, cols] = merged.astype(BF16)

    def out_piece(k):
        cols = slice(k * OUT_PIECE, (k + 1) * OUT_PIECE)
        o_ref[:, cols] = xprev_ref[:, cols] + jnp.dot(merged_ref[...], wo_ref[:, cols],
                                                      preferred_element_type=F32)

    n_pieces = merged_ref.shape[1] // OUT_PIECE
    pieces = ([functools.partial(merge_piece, k) for k in range(n_pieces)]
              + [functools.partial(out_piece, k) for k in range(n_pieces)])
    assert len(pieces) == 2 * HEADS
    sc = scores(0)
    for h in range(HEADS):
        sc_next = scores(h + 1) if h + 1 < HEADS else None
        pieces[2 * h]()
        attend(h, sc)
        retain(h, pieces[2 * h + 1])
        sc = sc_next


def _mixer_out(proj, x, rel_row, woa, wor, wo, batch, seq):
    t, d = x.shape
    tiles_per_seq = seq // TQ
    n_tiles = batch * tiles_per_seq
    proj4 = proj.reshape(proj.shape[0], batch, seq, WIDTH)

    def split(tt):
        return tt // tiles_per_seq, lax.rem(tt, tiles_per_seq)

    def cur(s):
        return split(jnp.minimum(s, n_tiles - 1))

    def tile(col):
        return pl.BlockSpec((None, None, TQ, WIDTH), lambda s: (col, *cur(s), 0))

    def prev_tile(col):
        return pl.BlockSpec((None, None, TQ, WIDTH),
                            lambda s: (col, *split(jnp.maximum(s - 1, 0)), 0))

    def hist(col, j):
        def index(s):
            b, i = cur(s)
            return col, b, jnp.maximum(i - (N_KBLK - 1) + j, 0), 0
        return pl.BlockSpec((None, None, TQ, WIDTH), index)

    def resident(shape):
        return pl.BlockSpec(shape, lambda s: (0, 0), pipeline_mode=pl.Buffered(1))

    prev = pl.BlockSpec((TQ, d), lambda s: (jnp.maximum(s - 1, 0), 0))
    gate0 = 7
    in_specs = ([tile(0)] + [hist(1, j) for j in range(N_KBLK)] + [hist(2, j) for j in range(N_KBLK)]
                + [tile(3), tile(4), tile(5), tile(6)]
                + [prev_tile(gate0 + c) for c in range(4)]
                + [resident(rel_row.shape), prev,
                   resident(woa.shape), resident(wor.shape), resident(wo.shape)])
    return pl.pallas_call(
        functools.partial(_mixer_out_kernel, n_tiles=n_tiles, tiles_per_seq=tiles_per_seq),
        grid=(n_tiles + 1,),
        in_specs=in_specs,
        out_specs=prev,
        out_shape=jax.ShapeDtypeStruct((t, d), F32),
        scratch_shapes=[pltpu.VMEM((HEADS + 1, TQ, N_KBLK * TQ), F32),
                        pltpu.VMEM((HEADS, TQ, TQ), F32),
                        pltpu.VMEM((HEADS, HEAD_DIM, HEAD_DIM), F32),
                        pltpu.VMEM((TQ, WIDTH), BF16),
                        pltpu.VMEM((TQ, WIDTH), BF16),
                        pltpu.VMEM((TQ, WIDTH), BF16),
                        pltpu.VMEM((TQ, WIDTH), BF16),
                        pltpu.VMEM((TQ, d), BF16)],
        compiler_params=pltpu.CompilerParams(
            dimension_semantics=("arbitrary",),
            vmem_limit_bytes=VMEM_LIMIT),
        name="mixer_out",
    )(*([proj4] * 15), rel_row, x, woa, wor, wo)


def _rotary_tables(seq):
    inv = 1.0 / (ROPE_BASE ** (np.arange(0, HEAD_DIM, 2, dtype=np.float64) / HEAD_DIM))
    ang = np.arange(seq, dtype=np.float64)[:, None] * inv[None, :]
    cos, sin = np.cos(ang), np.sin(ang)
    return (np.concatenate([cos, cos], axis=-1).astype(np.float32),
            np.concatenate([-sin, sin], axis=-1).astype(np.float32))


def _rel_row(rel_bias):
    j = np.arange((N_KBLK + 1) * TQ)
    rel_idx = np.clip(N_KBLK * TQ - j, -MAX_REL_DIST, MAX_REL_DIST) + MAX_REL_DIST
    return rel_bias.astype(F32)[:, rel_idx]


def kernel(x, norm_ffn1_g, ffn1_w_gate, ffn1_w_up, ffn1_w_down, norm_mix_g, w_in, rel_bias,
           w_out_att, w_out_ret, w_out, norm_ffn2_g, ffn2_w_gate, ffn2_w_up, ffn2_w_down,
           norm_final_g):
    batch, seq, d = x.shape
    depth = w_in.shape[0]
    assert depth >= 1 and (N_KBLK - 1) * TQ == N_PREV_CHUNKS * CHUNK and seq % TQ == 0
    cos2, sin2 = _rotary_tables(seq)
    xt = x.reshape(batch * seq, d)
    for l in range(depth):
        last = l == depth - 1
        g1 = norm_ffn1_g[l][None]
        y_lead, _, (wg1, wu1, wd1) = _ffn(xt, g1, ffn1_w_gate[l], ffn1_w_up[l], ffn1_w_down[l],
                                          n_tiles=1, tm=FFN_TM, tf=FFN_TF // 2)
        xt, (w_in_b, wg2, wu2, wd2), _ = _ffn(
            xt, g1, wg1, wu1, wd1, lead=y_lead, tm=FFN_TM, tf=FFN_TF,
            cast=((w_in[l], True), (ffn2_w_gate[l], True), (ffn2_w_up[l], True),
                  (ffn2_w_down[l], False)))
        proj, (woa, wor, wo) = _in_proj(
            xt, norm_mix_g[l][None], w_in_b, cos2, sin2, first_tile_scale=HEAD_DIM ** -0.5 * LOG2E,
            rot_tiles=((3, HEAD_DIM ** -0.5), (4, 1.0)),
            cast=((w_out_att[l], False), (w_out_ret[l], False), (w_out[l], False)))
        xt = _mixer_out(proj, xt, _rel_row(rel_bias[l]), woa, wor, wo, batch, seq)
        xt, _, _ = _ffn(xt, norm_ffn2_g[l][None], wg2, wu2, wd2, norm_final_g[None] if last else None,
                        tm=FFN_TM, tf=FFN_TF)
    return xt.reshape(batch, seq, d)
```

```python
import functools
import math

import numpy as np
import jax
import jax.numpy as jnp
from jax import lax
from jax.experimental import pallas as pl
from jax.experimental.pallas import tpu as pltpu

F32 = jnp.float32
BF16 = jnp.bfloat16

CHUNK = 64
CHUNK_SHIFT = CHUNK.bit_length() - 1
N_PREV_CHUNKS = 8
HEADS = 8
HEAD_DIM = 128
WIDTH = HEADS * HEAD_DIM
MAX_REL_DIST = 128
ROPE_BASE = 10000.0
EPS = 1e-6
NEG = -1e30
LOG2E = math.log2(math.e)

LANES = 128
BF16_SUBLANES = 16

TQ = 256
N_KBLK = 3
FFN_TM = 1024
FFN_TF = 512
FFN_PARTS = 2
W_RING = 3
MXU_WIDTH = 256
OUT_PIECE = MXU_WIDTH
MIB = 1024 * 1024
VMEM_LIMIT = 60 * MIB


def _rmsnorm(x, g):
    ms = jnp.mean(x * x, axis=-1, keepdims=True)
    return x * lax.rsqrt(ms + EPS) * g


def _silu(x):
    return x * jax.nn.sigmoid(x)


def _cast_plan(shape, n_i, n_j, blocked):
    r, c = shape
    same = lambda spec, every: (spec, spec, shape, every)
    if blocked:
        assert r % (n_i * BF16_SUBLANES) == 0 and c % (n_j * LANES) == 0, shape
        return (pl.BlockSpec((r // n_i, c // n_j), lambda i, j: (i, j)),
                pl.BlockSpec((None, r // n_i, c // n_j), lambda i, j: (j, i, 0)),
                (n_j, r, c // n_j), True)
    if r % (n_i * n_j * BF16_SUBLANES) == 0:
        return same(pl.BlockSpec((r // (n_i * n_j), c), lambda i, j: (i * n_j + j, 0)), True)
    if r % (n_i * BF16_SUBLANES) == 0 and c % (n_j * LANES) == 0:
        return same(pl.BlockSpec((r // n_i, c // n_j), lambda i, j: (i, j)), True)
    if r % (n_j * BF16_SUBLANES) == 0 and c % (n_i * LANES) == 0:
        return same(pl.BlockSpec((r // n_j, c // n_i), lambda i, j: (j, i)), True)
    if r % (n_i * BF16_SUBLANES) == 0:
        return same(pl.BlockSpec((r // n_i, c), lambda i, j: (i, 0)), False)
    rows = next(k for k in range(BF16_SUBLANES, r + 1, BF16_SUBLANES)
                if r % k == 0 and r // k <= n_i * n_j)
    last = r // rows - 1
    return same(pl.BlockSpec((rows, c), lambda i, j: (jnp.minimum(i * n_j + j, last), 0)), True)


def _cast_side_args(cast, n_i, n_j):
    plans = [_cast_plan(w.shape, n_i, n_j, blocked) for w, blocked in cast]
    order = ([k for k, p in enumerate(plans) if p[3]] + [k for k, p in enumerate(plans) if not p[3]])
    in_specs = [plans[k][0] for k in order]
    out_specs = [plans[k][1] for k in order]
    out_shapes = [jax.ShapeDtypeStruct(plans[k][2], BF16) for k in order]
    n_every = sum(1 for p in plans if p[3])
    return order, in_specs, out_specs, out_shapes, n_every


def _unpermute(values, order):
    out = [None] * len(order)
    for pos, k in enumerate(order):
        out[k] = values[pos]
    return out


def _cast_blocks(pairs):
    for src_ref, dst_ref in pairs:
        dst_ref[...] = src_ref[...].astype(BF16)


def _ffn_kernel(*refs, n_f, final_norm, own_cast, has_lead, n_cast, n_cast_every, n_parts):
    n_in = 6 if final_norm else 5
    n_own = 3 if own_cast else 0
    x_ref, g_ref = refs[:2]
    w_refs = refs[2:5]
    gf_ref = refs[5] if final_norm else None
    lead_ref = refs[n_in + n_cast] if has_lead else None
    n_inputs = n_in + n_cast + (1 if has_lead else 0)
    o_ref = refs[n_inputs]
    own_refs = refs[n_inputs + 1:n_inputs + 1 + n_own]
    casts = list(zip(refs[n_in:n_in + n_cast],
                     refs[n_inputs + 1 + n_own:n_inputs + 1 + n_own + n_cast]))
    cast_every, cast_first = casts[:n_cast_every], casts[n_cast_every:]
    h_ref = refs[n_inputs + 1 + n_own + n_cast]
    i = pl.program_id(0)
    j = pl.program_id(1)

    def partial_down(h, n_parts):
        if own_cast:
            _cast_blocks(zip(w_refs, own_refs))
        wg_ref, wu_ref, wd_ref = own_refs if own_cast else w_refs
        width = wd_ref.shape[0] // n_parts
        acts = []
        for p in range(n_parts):
            c = slice(p * width, (p + 1) * width)
            gate = jnp.dot(h, wg_ref[:, c], preferred_element_type=F32)
            up = jnp.dot(h, wu_ref[:, c], preferred_element_type=F32)
            acts.append((_silu(gate) * up).astype(BF16))
        downs = [jnp.dot(act, wd_ref[p * width:(p + 1) * width, :], preferred_element_type=F32)
                 for p, act in enumerate(acts)]
        return functools.reduce(jnp.add, downs)

    def when(cond):
        return pl.when(cond & (i > 0)) if has_lead else pl.when(cond)

    @when(j == 0)
    def _():
        _cast_blocks(cast_every + cast_first)
        h = _rmsnorm(x_ref[...], g_ref[...]).astype(BF16)
        h_ref[...] = h
        o_ref[...] = partial_down(h, n_parts)

    @when((j > 0) & (j < n_f - 1))
    def _():
        _cast_blocks(cast_every)
        o_ref[...] += partial_down(h_ref[...], n_parts)

    @when(j == n_f - 1)
    def _():
        _cast_blocks(cast_every)
        y = x_ref[...] + 0.5 * (o_ref[...] + partial_down(h_ref[...], 1))
        if final_norm:
            y = _rmsnorm(y, gf_ref[...])
        o_ref[...] = y

    if has_lead:
        @pl.when((i == 0) & (j == 0))
        def _():
            _cast_blocks(cast_every + cast_first)
            copy = pltpu.make_async_copy(lead_ref, o_ref, refs[-1])
            copy.start()
            copy.wait()

        @pl.when((i == 0) & (j > 0))
        def _():
            _cast_blocks(cast_every)


def _ffn(x, g, wg, wu, wd, g_final=None, *, tm, tf, cast=(), n_tiles=None, lead=None):
    t, d = x.shape
    n_i = t // tm if n_tiles is None else n_tiles
    n_f = wd.shape[0] // tf
    assert n_f >= 2
    final_norm = g_final is not None
    own_cast = wg.dtype == F32
    has_lead = lead is not None
    assert not (own_cast and has_lead)

    def fstep(i, j):
        return jnp.where(i == 0, 0, j) if has_lead else j

    def col_tile(w):
        if w.ndim == 3:
            return pl.BlockSpec((None, d, tf), lambda i, j: (fstep(i, j), 0, 0))
        return pl.BlockSpec((d, tf), lambda i, j: (0, fstep(i, j)))

    w_specs = [col_tile(wg), col_tile(wu), pl.BlockSpec((tf, d), lambda i, j: (fstep(i, j), 0))]
    row_spec = pl.BlockSpec((tm, d), lambda i, j: (i, 0))
    in_specs = [row_spec, pl.BlockSpec((1, d), lambda i, j: (0, 0))] + w_specs
    args = [x, g, wg, wu, wd]
    if final_norm:
        in_specs.append(pl.BlockSpec((1, d), lambda i, j: (0, 0)))
        args.append(g_final)
    order, cast_in, cast_out, cast_shapes, n_every = _cast_side_args(cast, n_i, n_f)
    in_specs += cast_in
    args += [cast[k][0] for k in order]
    scratch = [pltpu.VMEM((tm, d), BF16)]
    if has_lead:
        assert lead.shape == (tm, d)
        in_specs.append(pl.BlockSpec(memory_space=pl.ANY))
        args.append(lead)
        scratch.append(pltpu.SemaphoreType.DMA(()))
    own_specs = w_specs if own_cast else []
    own_shapes = [jax.ShapeDtypeStruct(w.shape, BF16) for w in (wg, wu, wd)] if own_cast else []
    outs = pl.pallas_call(
        functools.partial(_ffn_kernel, n_f=n_f, final_norm=final_norm, own_cast=own_cast,
                          has_lead=has_lead, n_cast=len(cast), n_cast_every=n_every,
                          n_parts=FFN_PARTS if tf % (FFN_PARTS * MXU_WIDTH) == 0 else 1),
        grid=(n_i, n_f),
        in_specs=in_specs,
        out_specs=[row_spec] + own_specs + cast_out,
        out_shape=[jax.ShapeDtypeStruct((n_i * tm, d), F32)] + own_shapes + cast_shapes,
        scratch_shapes=scratch,
        compiler_params=pltpu.CompilerParams(
            dimension_semantics=("arbitrary", "arbitrary"),
            vmem_limit_bytes=VMEM_LIMIT),
        name="ffn_final" if final_norm else ("ffn_lead" if own_cast else "ffn"),
    )(*args)
    n_own = len(own_shapes)
    return outs[0], _unpermute(outs[1 + n_own:], order), list(outs[1:1 + n_own])


def _in_proj_kernel(*refs, first_tile_scale, rot_tiles, n_cast, n_cast_every, n_i, n_j):
    x_hbm_ref, g_ref, w_hbm_ref, cos_ref, sin_ref = refs[:5]
    o_ref = refs[5 + n_cast]
    h_ref, wbuf_ref, wsem_ref, xbuf_ref, xsem_ref = refs[-5:]
    casts = list(zip(refs[5:5 + n_cast], refs[6 + n_cast:6 + 2 * n_cast]))
    cast_every, cast_first = casts[:n_cast_every], casts[n_cast_every:]
    i = pl.program_id(0)
    j = pl.program_id(1)
    step = i * n_j + j
    n_steps = n_i * n_j

    def rem(a, b):
        return a % b if isinstance(a, int) else lax.rem(a, b)

    def tile_copy(s):
        slot = rem(s, W_RING)
        return pltpu.make_async_copy(w_hbm_ref.at[rem(s, n_j)], wbuf_ref.at[slot], wsem_ref.at[slot])

    def x_copy(tile):
        slot = rem(tile, 2)
        return pltpu.make_async_copy(x_hbm_ref.at[tile], xbuf_ref.at[slot], xsem_ref.at[slot])

    @pl.when(step == 0)
    def _():
        x_copy(0).start()
        for s in range(W_RING - 1):
            tile_copy(s).start()

    @pl.when((j == 0) & (i + 1 < n_i))
    def _():
        x_copy(i + 1).start()

    @pl.when(step + (W_RING - 1) < n_steps)
    def _():
        tile_copy(step + (W_RING - 1)).start()

    tile_copy(step).wait()
    w_ref = wbuf_ref.at[rem(step, W_RING)]

    def project(h):
        return jnp.dot(h, w_ref[...], preferred_element_type=F32)

    @pl.when(j == 0)
    def _():
        _cast_blocks(cast_every + cast_first)
        x_copy(i).wait()
        h = _rmsnorm(xbuf_ref[rem(i, 2)], g_ref[...]).astype(BF16)
        h_ref[...] = h
        o_ref[...] = (project(h) * first_tile_scale).astype(BF16)

    plain = j > 0
    for tile, scale in rot_tiles:
        plain = plain & (j != tile)

        @pl.when(j == tile)
        def _(scale=scale):
            _cast_blocks(cast_every)
            acc = project(h_ref[...])
            cos = cos_ref[...]
            sin = sin_ref[...]
            for hd in range(o_ref.shape[1] // HEAD_DIM):
                cols = slice(hd * HEAD_DIM, (hd + 1) * HEAD_DIM)
                xh = acc[:, cols]
                rot = xh * cos + pltpu.roll(xh, HEAD_DIM // 2, axis=1) * sin
                o_ref[:, cols] = (rot * scale).astype(BF16)

    @pl.when(plain)
    def _():
        _cast_blocks(cast_every)
        o_ref[...] = project(h_ref[...]).astype(BF16)


def _in_proj(x, g, w, cos2, sin2, *, first_tile_scale, rot_tiles, cast=(), tm=1024, tn=WIDTH):
    t, d = x.shape
    n_j = w.shape[0]
    assert w.shape[1:] == (d, tn)
    n_i = t // tm
    seq_tiles = cos2.shape[0] // tm
    order, cast_in, cast_out, cast_shapes, n_every = _cast_side_args(cast, n_i, n_j)
    table = pl.BlockSpec((tm, HEAD_DIM), lambda i, j: (lax.rem(i, seq_tiles), 0))
    outs = pl.pallas_call(
        functools.partial(_in_proj_kernel, first_tile_scale=first_tile_scale, rot_tiles=rot_tiles,
                          n_cast=len(cast), n_cast_every=n_every, n_i=n_i, n_j=n_j),
        grid=(n_i, n_j),
        in_specs=[
            pl.BlockSpec(memory_space=pl.ANY),
            pl.BlockSpec((1, d), lambda i, j: (0, 0)),
            pl.BlockSpec(memory_space=pl.ANY),
            table, table,
        ] + cast_in,
        out_specs=[pl.BlockSpec((None, tm, tn), lambda i, j: (j, i, 0))] + cast_out,
        out_shape=[jax.ShapeDtypeStruct((n_j, t, tn), BF16)] + cast_shapes,
        scratch_shapes=[pltpu.VMEM((tm, d), BF16), pltpu.VMEM((W_RING, d, tn), BF16),
                        pltpu.SemaphoreType.DMA((W_RING,)),
                        pltpu.VMEM((2, tm, d), F32), pltpu.SemaphoreType.DMA((2,))],
        compiler_params=pltpu.CompilerParams(
            dimension_semantics=("arbitrary", "arbitrary"),
            vmem_limit_bytes=VMEM_LIMIT),
        name="in_proj",
    )(x.reshape(n_i, tm, d), g, w, cos2, sin2, *[cast[k][0] for k in order])
    return outs[0], _unpermute(outs[1:], order)


_LOG_GAMMA = tuple(math.log(1.0 - 2.0 ** (-5.0 - h)) for h in range(HEADS))


def _dot_nt(a, b):
    return lax.dot_general(a, b, (((1,), (1,)), ((), ())), preferred_element_type=F32)


def _lane_slabs(x):
    return [x[:, k:k + LANES] for k in range(0, x.shape[1], LANES)]


def _dot_tn(a, b):
    return lax.dot_general(a, b, (((0,), (0,)), ((), ())), preferred_element_type=F32)


def _mixer_out_kernel(qa_ref, ka0_ref, ka1_ref, ka2_ref, va0_ref, va1_ref, va2_ref,
                      qr_ref, kr_ref, vr_ref, gr_ref, ga0_ref, ga1_ref, gm0_ref, gm1_ref,
                      relrow_ref, xprev_ref, woa_ref, wor_ref, wo_ref,
                      o_ref,
                      bias_ref, decay_ref, state_ref,
                      att_ref, ret_ref, attp_ref, retp_ref, merged_ref, *, n_tiles, tiles_per_seq):
    s = pl.program_id(0)
    t = jnp.minimum(s, n_tiles - 1)
    i = lax.rem(t, tiles_per_seq)
    ka_refs = (ka0_ref, ka1_ref, ka2_ref)
    va_refs = (va0_ref, va1_ref, va2_ref)

    @pl.when(s == 0)
    def _():
        qc = lax.broadcasted_iota(jnp.int32, (TQ, N_KBLK * TQ), 0) >> CHUNK_SHIFT
        kc = lax.broadcasted_iota(jnp.int32, (TQ, N_KBLK * TQ), 1) >> CHUNK_SHIFT
        in_band = (kc >= qc) & (kc <= qc + N_PREV_CHUNKS)
        for h in range(HEADS):
            rows = jnp.broadcast_to(relrow_ref[h:h + 1, :], (TQ, (N_KBLK + 1) * TQ))
            toeplitz = pltpu.roll(rows, 0, axis=1, stride=1, stride_axis=0)
            bias_ref[h] = jnp.where(in_band, toeplitz[:, TQ:] * LOG2E, NEG)
        bias_ref[HEADS] = jnp.full((TQ, N_KBLK * TQ), NEG, F32)

        n = lax.broadcasted_iota(jnp.int32, (TQ, TQ), 0)
        m = lax.broadcasted_iota(jnp.int32, (TQ, TQ), 1)
        dist = jnp.abs(n - m).astype(F32)
        visible = (m >> CHUNK_SHIFT) <= (n >> CHUNK_SHIFT)
        for h in range(HEADS):
            decay_ref[h] = jnp.where(visible, jnp.exp(_LOG_GAMMA[h] * dist), 0.0)
        att_ref[...] = jnp.zeros_like(att_ref)
        ret_ref[...] = jnp.zeros_like(ret_ref)

    @pl.when(i == 0)
    def _():
        state_ref[...] = jnp.zeros_like(state_ref)

    attp_ref[...] = att_ref[...]
    retp_ref[...] = ret_ref[...]

    row = lax.broadcasted_iota(jnp.int32, (TQ, HEAD_DIM), 0).astype(F32)

    def scores(h):
        cols = slice(h * HEAD_DIM, (h + 1) * HEAD_DIM)
        q = qa_ref[:, cols]
        sc = []
        for j in range(N_KBLK):
            plane = h if j == N_KBLK - 1 else jnp.where(i + j >= N_KBLK - 1, h, HEADS)
            sc.append(_dot_nt(q, ka_refs[j][:, cols]) + bias_ref[plane, :, j * TQ:(j + 1) * TQ])
        return sc

    def attend(h, sc):
        cols = slice(h * HEAD_DIM, (h + 1) * HEAD_DIM)
        mx = jnp.maximum(jnp.maximum(jnp.max(sc[0], axis=-1, keepdims=True),
                                     jnp.max(sc[1], axis=-1, keepdims=True)),
                         jnp.max(sc[2], axis=-1, keepdims=True))
        acc = None
        den = None
        for j in range(N_KBLK):
            p = jnp.exp2(sc[j] - mx)
            dj = jnp.sum(p, axis=-1, keepdims=True)
            oj = jnp.dot(p.astype(BF16), va_refs[j][:, cols], preferred_element_type=F32)
            acc = oj if acc is None else acc + oj
            den = dj if den is None else den + dj
        att_ref[:, cols] = (acc / den).astype(BF16)

    def retain(h, fill):
        cols = slice(h * HEAD_DIM, (h + 1) * HEAD_DIM)
        qr = qr_ref[:, cols]
        kr = kr_ref[:, cols]
        kr_f = kr.astype(F32)
        v = vr_ref[:, cols]
        sr = _dot_nt(qr, kr)
        st = state_ref[h]
        lg = _LOG_GAMMA[h]
        cross = jnp.dot(qr, st.astype(BF16), preferred_element_type=F32)
        fill()
        inner = jnp.dot((sr * decay_ref[h]).astype(BF16), v, preferred_element_type=F32)
        kd = (kr_f * jnp.exp(lg * (TQ - 1.0 - row))).astype(BF16)
        state_ref[h] = st * math.exp(lg * TQ) + _dot_tn(kd, v)
        out = inner + cross * jnp.exp(lg * (row + 1.0))
        out = out * lax.rsqrt(jnp.mean(out * out, axis=-1, keepdims=True) + EPS)
        ret_ref[:, cols] = (_silu(gr_ref[:, cols].astype(F32)) * out).astype(BF16)

    gate_refs = ((ga0_ref, gm0_ref), (ga1_ref, gm1_ref))

    def merge_piece(k):
        cols = slice(k * OUT_PIECE, (k + 1) * OUT_PIECE)
        ga_ref, gm_ref = gate_refs[k * OUT_PIECE // WIDTH]
        gcols = slice(k * OUT_PIECE % WIDTH, k * OUT_PIECE % WIDTH + OUT_PIECE)
        branch_a = jnp.dot(attp_ref[...], woa_ref[:, cols], preferred_element_type=F32)
        branch_r = jnp.dot(retp_ref[...], wor_ref[:, cols], preferred_element_type=F32)
        merged = (jax.nn.sigmoid(ga_ref[:, gcols].astype(F32)) * branch_a
                  + jax.nn.sigmoid(gm_ref[:, gcols].astype(F32)) * branch_r)
        merged_ref[:, cols] = merged.astype(BF16)

    def out_piece(k):
        cols = slice(k * OUT_PIECE, (k + 1) * OUT_PIECE)
        o_ref[:, cols] = xprev_ref[:, cols] + jnp.dot(merged_ref[...], wo_ref[:, cols],
                                                      preferred_element_type=F32)

    n_pieces = merged_ref.shape[1] // OUT_PIECE
    pieces = ([functools.partial(merge_piece, k) for k in range(n_pieces)]
              + [functools.partial(out_piece, k) for k in range(n_pieces)])
    assert len(pieces) == 2 * HEADS
    sc = scores(0)
    for h in range(HEADS):
        sc_next = scores(h + 1) if h + 1 < HEADS else None
        pieces[2 * h]()
        attend(h, sc)
        retain(h, pieces[2 * h + 1])
        sc = sc_next


def _mixer_out(proj, x, rel_row, woa, wor, wo, batch, seq):
    t, d = x.shape
    tiles_per_seq = seq // TQ
    n_tiles = batch * tiles_per_seq
    proj4 = proj.reshape(proj.shape[0], batch, seq, WIDTH)

    def split(tt):
        return tt // tiles_per_seq, lax.rem(tt, tiles_per_seq)

    def cur(s):
        return split(jnp.minimum(s, n_tiles - 1))

    def tile(col):
        return pl.BlockSpec((None, None, TQ, WIDTH), lambda s: (col, *cur(s), 0))

    def prev_tile(col):
        return pl.BlockSpec((None, None, TQ, WIDTH),
                            lambda s: (col, *split(jnp.maximum(s - 1, 0)), 0))

    def hist(col, j):
        def index(s):
            b, i = cur(s)
            return col, b, jnp.maximum(i - (N_KBLK - 1) + j, 0), 0
        return pl.BlockSpec((None, None, TQ, WIDTH), index)

    def resident(shape):
        return pl.BlockSpec(shape, lambda s: (0, 0), pipeline_mode=pl.Buffered(1))

    prev = pl.BlockSpec((TQ, d), lambda s: (jnp.maximum(s - 1, 0), 0))
    gate0 = 7
    in_specs = ([tile(0)] + [hist(1, j) for j in range(N_KBLK)] + [hist(2, j) for j in range(N_KBLK)]
                + [tile(3), tile(4), tile(5), tile(6)]
                + [prev_tile(gate0 + c) for c in range(4)]
                + [resident(rel_row.shape), prev,
                   resident(woa.shape), resident(wor.shape), resident(wo.shape)])
    return pl.pallas_call(
        functools.partial(_mixer_out_kernel, n_tiles=n_tiles, tiles_per_seq=tiles_per_seq),
        grid=(n_tiles + 1,),
        in_specs=in_specs,
        out_specs=prev,
        out_shape=jax.ShapeDtypeStruct((t, d), F32),
        scratch_shapes=[pltpu.VMEM((HEADS + 1, TQ, N_KBLK * TQ), F32),
                        pltpu.VMEM((HEADS, TQ, TQ), F32),
                        pltpu.VMEM((HEADS, HEAD_DIM, HEAD_DIM), F32),
                        pltpu.VMEM((TQ, WIDTH), BF16),
                        pltpu.VMEM((TQ, WIDTH), BF16),
                        pltpu.VMEM((TQ, WIDTH), BF16),
                        pltpu.VMEM((TQ, WIDTH), BF16),
                        pltpu.VMEM((TQ, d), BF16)],
        compiler_params=pltpu.CompilerParams(
            dimension_semantics=("arbitrary",),
            vmem_limit_bytes=VMEM_LIMIT),
        name="mixer_out",
    )(*([proj4] * 15), rel_row, x, woa, wor, wo)


def _rotary_tables(seq):
    inv = 1.0 / (ROPE_BASE ** (np.arange(0, HEAD_DIM, 2, dtype=np.float64) / HEAD_DIM))
    ang = np.arange(seq, dtype=np.float64)[:, None] * inv[None, :]
    cos, sin = np.cos(ang), np.sin(ang)
    return (np.concatenate([cos, cos], axis=-1).astype(np.float32),
            np.concatenate([-sin, sin], axis=-1).astype(np.float32))


def _rel_row(rel_bias):
    j = np.arange((N_KBLK + 1) * TQ)
    rel_idx = np.clip(N_KBLK * TQ - j, -MAX_REL_DIST, MAX_REL_DIST) + MAX_REL_DIST
    return rel_bias.astype(F32)[:, rel_idx]


def kernel(x, norm_ffn1_g, ffn1_w_gate, ffn1_w_up, ffn1_w_down, norm_mix_g, w_in, rel_bias,
           w_out_att, w_out_ret, w_out, norm_ffn2_g, ffn2_w_gate, ffn2_w_up, ffn2_w_down,
           norm_final_g):
    batch, seq, d = x.shape
    depth = w_in.shape[0]
    assert depth >= 1 and (N_KBLK - 1) * TQ == N_PREV_CHUNKS * CHUNK and seq % TQ == 0
    cos2, sin2 = _rotary_tables(seq)
    xt = x.reshape(batch * seq, d)
    for l in range(depth):
        last = l == depth - 1
        g1 = norm_ffn1_g[l][None]
        y_lead, _, (wg1, wu1, wd1) = _ffn(xt, g1, ffn1_w_gate[l], ffn1_w_up[l], ffn1_w_down[l],
                                          n_tiles=1, tm=FFN_TM, tf=FFN_TF // 2)
        xt, (w_in_b, wg2, wu2, wd2), _ = _ffn(
            xt, g1, wg1, wu1, wd1, lead=y_lead, tm=FFN_TM, tf=FFN_TF,
            cast=((w_in[l], True), (ffn2_w_gate[l], True), (ffn2_w_up[l], True),
                  (ffn2_w_down[l], False)))
        proj, (woa, wor, wo) = _in_proj(
            xt, norm_mix_g[l][None], w_in_b, cos2, sin2, first_tile_scale=HEAD_DIM ** -0.5 * LOG2E,
            rot_tiles=((3, HEAD_DIM ** -0.5), (4, 1.0)),
            cast=((w_out_att[l], False), (w_out_ret[l], False), (w_out[l], False)))
        xt = _mixer_out(proj, xt, _rel_row(rel_bias[l]), woa, wor, wo, batch, seq)
        xt, _, _ = _ffn(xt, norm_ffn2_g[l][None], wg2, wu2, wd2, norm_final_g[None] if last else None,
                        tm=FFN_TM, tf=FFN_TF)
    return xt.reshape(batch, seq, d)
```

```python
import functools
import math

import numpy as np
import jax
import jax.numpy as jnp
from jax import lax
from jax.experimental import pallas as pl
from jax.experimental.pallas import tpu as pltpu

F32 = jnp.float32
BF16 = jnp.bfloat16

CHUNK = 64
CHUNK_SHIFT = CHUNK.bit_length() - 1
N_PREV_CHUNKS = 8
HEADS = 8
HEAD_DIM = 128
WIDTH = HEADS * HEAD_DIM
MAX_REL_DIST = 128
ROPE_BASE = 10000.0
EPS = 1e-6
NEG = -1e30
LOG2E = math.log2(math.e)

LANES = 128
BF16_SUBLANES = 16

TQ = 256
N_KBLK = 3
FFN_TM = 1024
FFN_TF = 512
FFN_PARTS = 2
W_RING = 3
MXU_WIDTH = 256
OUT_PIECE = MXU_WIDTH
MIB = 1024 * 1024
VMEM_LIMIT = 60 * MIB


def _rmsnorm(x, g):
    ms = jnp.mean(x * x, axis=-1, keepdims=True)
    return x * lax.rsqrt(ms + EPS) * g


def _silu(x):
    return x * jax.nn.sigmoid(x)


def _cast_plan(shape, n_i, n_j, blocked):
    r, c = shape
    same = lambda spec, every: (spec, spec, shape, every)
    if blocked:
        assert r % (n_i * BF16_SUBLANES) == 0 and c % (n_j * LANES) == 0, shape
        return (pl.BlockSpec((r // n_i, c // n_j), lambda i, j: (i, j)),
                pl.BlockSpec((None, r // n_i, c // n_j), lambda i, j: (j, i, 0)),
                (n_j, r, c // n_j), True)
    if r % (n_i * n_j * BF16_SUBLANES) == 0:
        return same(pl.BlockSpec((r // (n_i * n_j), c), lambda i, j: (i * n_j + j, 0)), True)
    if r % (n_i * BF16_SUBLANES) == 0 and c % (n_j * LANES) == 0:
        return same(pl.BlockSpec((r // n_i, c // n_j), lambda i, j: (i, j)), True)
    if r % (n_j * BF16_SUBLANES) == 0 and c % (n_i * LANES) == 0:
        return same(pl.BlockSpec((r // n_j, c // n_i), lambda i, j: (j, i)), True)
    if r % (n_i * BF16_SUBLANES) == 0:
        return same(pl.BlockSpec((r // n_i, c), lambda i, j: (i, 0)), False)
    rows = next(k for k in range(BF16_SUBLANES, r + 1, BF16_SUBLANES)
                if r % k == 0 and r // k <= n_i * n_j)
    last = r // rows - 1
    return same(pl.BlockSpec((rows, c), lambda i, j: (jnp.minimum(i * n_j + j, last), 0)), True)


def _cast_side_args(cast, n_i, n_j):
    plans = [_cast_plan(w.shape, n_i, n_j, blocked) for w, blocked in cast]
    order = ([k for k, p in enumerate(plans) if p[3]] + [k for k, p in enumerate(plans) if not p[3]])
    in_specs = [plans[k][0] for k in order]
    out_specs = [plans[k][1] for k in order]
    out_shapes = [jax.ShapeDtypeStruct(plans[k][2], BF16) for k in order]
    n_every = sum(1 for p in plans if p[3])
    return order, in_specs, out_specs, out_shapes, n_every


def _unpermute(values, order):
    out = [None] * len(order)
    for pos, k in enumerate(order):
        out[k] = values[pos]
    return out


def _cast_blocks(pairs):
    for src_ref, dst_ref in pairs:
        dst_ref[...] = src_ref[...].astype(BF16)


def _ffn_kernel(*refs, n_i, n_f, final_norm, own_cast, has_lead, n_cast, n_cast_every, n_parts):
    n_in = 6 if final_norm else 5
    n_own = 3 if own_cast else 0
    x_hbm_ref, g_ref = refs[:2]
    w_refs = refs[2:5]
    gf_ref = refs[5] if final_norm else None
    lead_ref = refs[n_in + n_cast] if has_lead else None
    n_inputs = n_in + n_cast + (1 if has_lead else 0)
    o_ref = refs[n_inputs]
    own_refs = refs[n_inputs + 1:n_inputs + 1 + n_own]
    casts = list(zip(refs[n_in:n_in + n_cast],
                     refs[n_inputs + 1 + n_own:n_inputs + 1 + n_own + n_cast]))
    cast_every, cast_first = casts[:n_cast_every], casts[n_cast_every:]
    h_ref, xbuf_ref, xsem_ref = refs[n_inputs + 1 + n_own + n_cast:n_inputs + 4 + n_own + n_cast]
    i = pl.program_id(0)
    j = pl.program_id(1)
    first = 1 if has_lead else 0

    def x_copy(tile):
        slot = lax.rem(tile, 2)
        return pltpu.make_async_copy(x_hbm_ref.at[tile], xbuf_ref.at[slot], xsem_ref.at[slot])

    @pl.when((i == 0) & (j == 0))
    def _():
        x_copy(jnp.int32(first)).start()

    @pl.when((j == 0) & (i >= first) & (i + 1 < n_i))
    def _():
        x_copy(i + 1).start()

    x_ref = xbuf_ref.at[lax.rem(i, 2)]

    def partial_down(h, n_parts):
        if own_cast:
            _cast_blocks(zip(w_refs, own_refs))
        wg_ref, wu_ref, wd_ref = own_refs if own_cast else w_refs
        width = wd_ref.shape[0] // n_parts
        acts = []
        for p in range(n_parts):
            c = slice(p * width, (p + 1) * width)
            gate = jnp.dot(h, wg_ref[:, c], preferred_element_type=F32)
            up = jnp.dot(h, wu_ref[:, c], preferred_element_type=F32)
            acts.append((_silu(gate) * up).astype(BF16))
        downs = [jnp.dot(act, wd_ref[p * width:(p + 1) * width, :], preferred_element_type=F32)
                 for p, act in enumerate(acts)]
        return functools.reduce(jnp.add, downs)

    def when(cond):
        return pl.when(cond & (i > 0)) if has_lead else pl.when(cond)

    @when(j == 0)
    def _():
        _cast_blocks(cast_every + cast_first)
        x_copy(i).wait()
        h = _rmsnorm(x_ref[...], g_ref[...]).astype(BF16)
        h_ref[...] = h
        o_ref[...] = partial_down(h, n_parts)

    @when((j > 0) & (j < n_f - 1))
    def _():
        _cast_blocks(cast_every)
        o_ref[...] += partial_down(h_ref[...], n_parts)

    @when(j == n_f - 1)
    def _():
        _cast_blocks(cast_every)
        y = x_ref[...] + 0.5 * (o_ref[...] + partial_down(h_ref[...], 1))
        if final_norm:
            y = _rmsnorm(y, gf_ref[...])
        o_ref[...] = y

    if has_lead:
        @pl.when((i == 0) & (j == 0))
        def _():
            _cast_blocks(cast_every + cast_first)
            copy = pltpu.make_async_copy(lead_ref, o_ref, refs[-1])
            copy.start()
            copy.wait()

        @pl.when((i == 0) & (j > 0))
        def _():
            _cast_blocks(cast_every)


def _ffn(x, g, wg, wu, wd, g_final=None, *, tm, tf, cast=(), n_tiles=None, lead=None):
    t, d = x.shape
    n_i = t // tm if n_tiles is None else n_tiles
    n_f = wd.shape[0] // tf
    assert n_f >= 2
    final_norm = g_final is not None
    own_cast = wg.dtype == F32
    has_lead = lead is not None
    assert not (own_cast and has_lead)

    def fstep(i, j):
        return jnp.where(i == 0, 0, j) if has_lead else j

    def col_tile(w):
        if w.ndim == 3:
            return pl.BlockSpec((None, d, tf), lambda i, j: (fstep(i, j), 0, 0))
        return pl.BlockSpec((d, tf), lambda i, j: (0, fstep(i, j)))

    w_specs = [col_tile(wg), col_tile(wu), pl.BlockSpec((tf, d), lambda i, j: (fstep(i, j), 0))]
    row_spec = pl.BlockSpec((tm, d), lambda i, j: (i, 0))
    in_specs = [pl.BlockSpec(memory_space=pl.ANY), pl.BlockSpec((1, d), lambda i, j: (0, 0))] + w_specs
    args = [x.reshape(t // tm, tm, d), g, wg, wu, wd]
    if final_norm:
        in_specs.append(pl.BlockSpec((1, d), lambda i, j: (0, 0)))
        args.append(g_final)
    order, cast_in, cast_out, cast_shapes, n_every = _cast_side_args(cast, n_i, n_f)
    in_specs += cast_in
    args += [cast[k][0] for k in order]
    scratch = [pltpu.VMEM((tm, d), BF16), pltpu.VMEM((2, tm, d), F32), pltpu.SemaphoreType.DMA((2,))]
    if has_lead:
        assert lead.shape == (tm, d)
        in_specs.append(pl.BlockSpec(memory_space=pl.ANY))
        args.append(lead)
        scratch.append(pltpu.SemaphoreType.DMA(()))
    own_specs = w_specs if own_cast else []
    own_shapes = [jax.ShapeDtypeStruct(w.shape, BF16) for w in (wg, wu, wd)] if own_cast else []
    outs = pl.pallas_call(
        functools.partial(_ffn_kernel, n_i=n_i, n_f=n_f, final_norm=final_norm, own_cast=own_cast,
                          has_lead=has_lead, n_cast=len(cast), n_cast_every=n_every,
                          n_parts=FFN_PARTS if tf % (FFN_PARTS * MXU_WIDTH) == 0 else 1),
        grid=(n_i, n_f),
        in_specs=in_specs,
        out_specs=[row_spec] + own_specs + cast_out,
        out_shape=[jax.ShapeDtypeStruct((n_i * tm, d), F32)] + own_shapes + cast_shapes,
        scratch_shapes=scratch,
        compiler_params=pltpu.CompilerParams(
            dimension_semantics=("arbitrary", "arbitrary"),
            vmem_limit_bytes=VMEM_LIMIT),
        name="ffn_final" if final_norm else ("ffn_lead" if own_cast else "ffn"),
    )(*args)
    n_own = len(own_shapes)
    return outs[0], _unpermute(outs[1 + n_own:], order), list(outs[1:1 + n_own])


def _in_proj_kernel(*refs, first_tile_scale, rot_tiles, n_cast, n_cast_every, n_i, n_j):
    x_hbm_ref, g_ref, w_hbm_ref, cos_ref, sin_ref = refs[:5]
    o_ref = refs[5 + n_cast]
    h_ref, wbuf_ref, wsem_ref, xbuf_ref, xsem_ref = refs[-5:]
    casts = list(zip(refs[5:5 + n_cast], refs[6 + n_cast:6 + 2 * n_cast]))
    cast_every, cast_first = casts[:n_cast_every], casts[n_cast_every:]
    i = pl.program_id(0)
    j = pl.program_id(1)
    step = i * n_j + j
    n_steps = n_i * n_j

    def rem(a, b):
        return a % b if isinstance(a, int) else lax.rem(a, b)

    def tile_copy(s):
        slot = rem(s, W_RING)
        return pltpu.make_async_copy(w_hbm_ref.at[rem(s, n_j)], wbuf_ref.at[slot], wsem_ref.at[slot])

    def x_copy(tile):
        slot = rem(tile, 2)
        return pltpu.make_async_copy(x_hbm_ref.at[tile], xbuf_ref.at[slot], xsem_ref.at[slot])

    @pl.when(step == 0)
    def _():
        x_copy(0).start()
        for s in range(W_RING - 1):
            tile_copy(s).start()

    @pl.when((j == 0) & (i + 1 < n_i))
    def _():
        x_copy(i + 1).start()

    @pl.when(step + (W_RING - 1) < n_steps)
    def _():
        tile_copy(step + (W_RING - 1)).start()

    tile_copy(step).wait()
    w_ref = wbuf_ref.at[rem(step, W_RING)]

    def project(h):
        return jnp.dot(h, w_ref[...], preferred_element_type=F32)

    @pl.when(j == 0)
    def _():
        _cast_blocks(cast_every + cast_first)
        x_copy(i).wait()
        h = _rmsnorm(xbuf_ref[rem(i, 2)], g_ref[...]).astype(BF16)
        h_ref[...] = h
        o_ref[...] = (project(h) * first_tile_scale).astype(BF16)

    plain = j > 0
    for tile, scale in rot_tiles:
        plain = plain & (j != tile)

        @pl.when(j == tile)
        def _(scale=scale):
            _cast_blocks(cast_every)
            acc = project(h_ref[...])
            cos = cos_ref[...]
            sin = sin_ref[...]
            for hd in range(o_ref.shape[1] // HEAD_DIM):
                cols = slice(hd * HEAD_DIM, (hd + 1) * HEAD_DIM)
                xh = acc[:, cols]
                rot = xh * cos + pltpu.roll(xh, HEAD_DIM // 2, axis=1) * sin
                o_ref[:, cols] = (rot * scale).astype(BF16)

    @pl.when(plain)
    def _():
        _cast_blocks(cast_every)
        o_ref[...] = project(h_ref[...]).astype(BF16)


def _in_proj(x, g, w, cos2, sin2, *, first_tile_scale, rot_tiles, cast=(), tm=1024, tn=WIDTH):
    t, d = x.shape
    n_j = w.shape[0]
    assert w.shape[1:] == (d, tn)
    n_i = t // tm
    seq_tiles = cos2.shape[0] // tm
    order, cast_in, cast_out, cast_shapes, n_every = _cast_side_args(cast, n_i, n_j)
    table = pl.BlockSpec((tm, HEAD_DIM), lambda i, j: (lax.rem(i, seq_tiles), 0))
    outs = pl.pallas_call(
        functools.partial(_in_proj_kernel, first_tile_scale=first_tile_scale, rot_tiles=rot_tiles,
                          n_cast=len(cast), n_cast_every=n_every, n_i=n_i, n_j=n_j),
        grid=(n_i, n_j),
        in_specs=[
            pl.BlockSpec(memory_space=pl.ANY),
            pl.BlockSpec((1, d), lambda i, j: (0, 0)),
            pl.BlockSpec(memory_space=pl.ANY),
            table, table,
        ] + cast_in,
        out_specs=[pl.BlockSpec((None, tm, tn), lambda i, j: (j, i, 0))] + cast_out,
        out_shape=[jax.ShapeDtypeStruct((n_j, t, tn), BF16)] + cast_shapes,
        scratch_shapes=[pltpu.VMEM((tm, d), BF16), pltpu.VMEM((W_RING, d, tn), BF16),
                        pltpu.SemaphoreType.DMA((W_RING,)),
                        pltpu.VMEM((2, tm, d), F32), pltpu.SemaphoreType.DMA((2,))],
        compiler_params=pltpu.CompilerParams(
            dimension_semantics=("arbitrary", "arbitrary"),
            vmem_limit_bytes=VMEM_LIMIT),
        name="in_proj",
    )(x.reshape(n_i, tm, d), g, w, cos2, sin2, *[cast[k][0] for k in order])
    return outs[0], _unpermute(outs[1:], order)


_LOG_GAMMA = tuple(math.log(1.0 - 2.0 ** (-5.0 - h)) for h in range(HEADS))


def _dot_nt(a, b):
    return lax.dot_general(a, b, (((1,), (1,)), ((), ())), preferred_element_type=F32)


def _lane_slabs(x):
    return [x[:, k:k + LANES] for k in range(0, x.shape[1], LANES)]


def _dot_tn(a, b):
    return lax.dot_general(a, b, (((0,), (0,)), ((), ())), preferred_element_type=F32)


def _mixer_out_kernel(qa_ref, ka0_ref, ka1_ref, ka2_ref, va0_ref, va1_ref, va2_ref,
                      qr_ref, kr_ref, vr_ref, gr_ref, ga0_ref, ga1_ref, gm0_ref, gm1_ref,
                      relrow_ref, xprev_ref, woa_ref, wor_ref, wo_ref,
                      o_ref,
                      bias_ref, decay_ref, state_ref,
                      att_ref, ret_ref, attp_ref, retp_ref, merged_ref, *, n_tiles, tiles_per_seq):
    s = pl.program_id(0)
    t = jnp.minimum(s, n_tiles - 1)
    i = lax.rem(t, tiles_per_seq)
    ka_refs = (ka0_ref, ka1_ref, ka2_ref)
    va_refs = (va0_ref, va1_ref, va2_ref)

    @pl.when(s == 0)
    def _():
        qc = lax.broadcasted_iota(jnp.int32, (TQ, N_KBLK * TQ), 0) >> CHUNK_SHIFT
        kc = lax.broadcasted_iota(jnp.int32, (TQ, N_KBLK * TQ), 1) >> CHUNK_SHIFT
        in_band = (kc >= qc) & (kc <= qc + N_PREV_CHUNKS)
        for h in range(HEADS):
            rows = jnp.broadcast_to(relrow_ref[h:h + 1, :], (TQ, (N_KBLK + 1) * TQ))
            toeplitz = pltpu.roll(rows, 0, axis=1, stride=1, stride_axis=0)
            bias_ref[h] = jnp.where(in_band, toeplitz[:, TQ:] * LOG2E, NEG)
        bias_ref[HEADS] = jnp.full((TQ, N_KBLK * TQ), NEG, F32)

        n = lax.broadcasted_iota(jnp.int32, (TQ, TQ), 0)
        m = lax.broadcasted_iota(jnp.int32, (TQ, TQ), 1)
        dist = jnp.abs(n - m).astype(F32)
        visible = (m >> CHUNK_SHIFT) <= (n >> CHUNK_SHIFT)
        for h in range(HEADS):
            decay_ref[h] = jnp.where(visible, jnp.exp(_LOG_GAMMA[h] * dist), 0.0)
        att_ref[...] = jnp.zeros_like(att_ref)
        ret_ref[...] = jnp.zeros_like(ret_ref)

    @pl.when(i == 0)
    def _():
        state_ref[...] = jnp.zeros_like(state_ref)

    attp_ref[...] = att_ref[...]
    retp_ref[...] = ret_ref[...]

    row = lax.broadcasted_iota(jnp.int32, (TQ, HEAD_DIM), 0).astype(F32)

    def scores(h):
        cols = slice(h * HEAD_DIM, (h + 1) * HEAD_DIM)
        q = qa_ref[:, cols]
        sc = []
        for j in range(N_KBLK):
            plane = h if j == N_KBLK - 1 else jnp.where(i + j >= N_KBLK - 1, h, HEADS)
            sc.append(_dot_nt(q, ka_refs[j][:, cols]) + bias_ref[plane, :, j * TQ:(j + 1) * TQ])
        return sc

    def attend(h, sc):
        cols = slice(h * HEAD_DIM, (h + 1) * HEAD_DIM)
        mx = jnp.maximum(jnp.maximum(jnp.max(sc[0], axis=-1, keepdims=True),
                                     jnp.max(sc[1], axis=-1, keepdims=True)),
                         jnp.max(sc[2], axis=-1, keepdims=True))
        acc = None
        den = None
        for j in range(N_KBLK):
            p = jnp.exp2(sc[j] - mx)
            dj = jnp.sum(p, axis=-1, keepdims=True)
            oj = jnp.dot(p.astype(BF16), va_refs[j][:, cols], preferred_element_type=F32)
            acc = oj if acc is None else acc + oj
            den = dj if den is None else den + dj
        att_ref[:, cols] = (acc / den).astype(BF16)

    def retain(h, fill):
        cols = slice(h * HEAD_DIM, (h + 1) * HEAD_DIM)
        qr = qr_ref[:, cols]
        kr = kr_ref[:, cols]
        kr_f = kr.astype(F32)
        v = vr_ref[:, cols]
        sr = _dot_nt(qr, kr)
        st = state_ref[h]
        lg = _LOG_GAMMA[h]
        cross = jnp.dot(qr, st.astype(BF16), preferred_element_type=F32)
        fill()
        inner = jnp.dot((sr * decay_ref[h]).astype(BF16), v, preferred_element_type=F32)
        kd = (kr_f * jnp.exp(lg * (TQ - 1.0 - row))).astype(BF16)
        state_ref[h] = st * math.exp(lg * TQ) + _dot_tn(kd, v)
        out = inner + cross * jnp.exp(lg * (row + 1.0))
        out = out * lax.rsqrt(jnp.mean(out * out, axis=-1, keepdims=True) + EPS)
        ret_ref[:, cols] = (_silu(gr_ref[:, cols].astype(F32)) * out).astype(BF16)

    gate_refs = ((ga0_ref, gm0_ref), (ga1_ref, gm1_ref))

    def merge_piece(k):
        cols = slice(k * OUT_PIECE, (k + 1) * OUT_PIECE)
        ga_ref, gm_ref = gate_refs[k * OUT_PIECE // WIDTH]
        gcols = slice(k * OUT_PIECE % WIDTH, k * OUT_PIECE % WIDTH + OUT_PIECE)
        branch_a = jnp.dot(attp_ref[...], woa_ref[:, cols], preferred_element_type=F32)
        branch_r = jnp.dot(retp_ref[...], wor_ref[:, cols], preferred_element_type=F32)
        merged = (jax.nn.sigmoid(ga_ref[:, gcols].astype(F32)) * branch_a
                  + jax.nn.sigmoid(gm_ref[:, gcols].astype(F32)) * branch_r)
        merged_ref[:, cols] = merged.astype(BF16)

    def out_piece(k):
        cols = slice(k * OUT_PIECE, (k + 1) * OUT_PIECE)
        o_ref[:, cols] = xprev_ref[:, cols] + jnp.dot(merged_ref[...], wo_ref[:, cols],
                                                      preferred_element_type=F32)

    n_pieces = merged_ref.shape[1] // OUT_PIECE
    pieces = ([functools.partial(merge_piece, k) for k in range(n_pieces)]
              + [functools.partial(out_piece, k) for k in range(n_pieces)])
    assert len(pieces) == 2 * HEADS
    sc = scores(0)
    for h in range(HEADS):
        sc_next = scores(h + 1) if h + 1 < HEADS else None
        pieces[2 * h]()
        attend(h, sc)
        retain(h, pieces[2 * h + 1])
        sc = sc_next


def _mixer_out(proj, x, rel_row, woa, wor, wo, batch, seq):
    t, d = x.shape
    tiles_per_seq = seq // TQ
    n_tiles = batch * tiles_per_seq
    proj4 = proj.reshape(proj.shape[0], batch, seq, WIDTH)

    def split(tt):
        return tt // tiles_per_seq, lax.rem(tt, tiles_per_seq)

    def cur(s):
        return split(jnp.minimum(s, n_tiles - 1))

    def tile(col):
        return pl.BlockSpec((None, None, TQ, WIDTH), lambda s: (col, *cur(s), 0))

    def prev_tile(col):
        return pl.BlockSpec((None, None, TQ, WIDTH),
                            lambda s: (col, *split(jnp.maximum(s - 1, 0)), 0))

    def hist(col, j):
        def index(s):
            b, i = cur(s)
            return col, b, jnp.maximum(i - (N_KBLK - 1) + j, 0), 0
        return pl.BlockSpec((None, None, TQ, WIDTH), index)

    def resident(shape):
        return pl.BlockSpec(shape, lambda s: (0, 0), pipeline_mode=pl.Buffered(1))

    prev = pl.BlockSpec((TQ, d), lambda s: (jnp.maximum(s - 1, 0), 0))
    gate0 = 7
    in_specs = ([tile(0)] + [hist(1, j) for j in range(N_KBLK)] + [hist(2, j) for j in range(N_KBLK)]
                + [tile(3), tile(4), tile(5), tile(6)]
                + [prev_tile(gate0 + c) for c in range(4)]
                + [resident(rel_row.shape), prev,
                   resident(woa.shape), resident(wor.shape), resident(wo.shape)])
    return pl.pallas_call(
        functools.partial(_mixer_out_kernel, n_tiles=n_tiles, tiles_per_seq=tiles_per_seq),
        grid=(n_tiles + 1,),
        in_specs=in_specs,
        out_specs=prev,
        out_shape=jax.ShapeDtypeStruct((t, d), F32),
        scratch_shapes=[pltpu.VMEM((HEADS + 1, TQ, N_KBLK * TQ), F32),
                        pltpu.VMEM((HEADS, TQ, TQ), F32),
                        pltpu.VMEM((HEADS, HEAD_DIM, HEAD_DIM), F32),
                        pltpu.VMEM((TQ, WIDTH), BF16),
                        pltpu.VMEM((TQ, WIDTH), BF16),
                        pltpu.VMEM((TQ, WIDTH), BF16),
                        pltpu.VMEM((TQ, WIDTH), BF16),
                        pltpu.VMEM((TQ, d), BF16)],
        compiler_params=pltpu.CompilerParams(
            dimension_semantics=("arbitrary",),
            vmem_limit_bytes=VMEM_LIMIT),
        name="mixer_out",
    )(*([proj4] * 15), rel_row, x, woa, wor, wo)


def _rotary_tables(seq):
    inv = 1.0 / (ROPE_BASE ** (np.arange(0, HEAD_DIM, 2, dtype=np.float64) / HEAD_DIM))
    ang = np.arange(seq, dtype=np.float64)[:, None] * inv[None, :]
    cos, sin = np.cos(ang), np.sin(ang)
    return (np.concatenate([cos, cos], axis=-1).astype(np.float32),
            np.concatenate([-sin, sin], axis=-1).astype(np.float32))


def _rel_row(rel_bias):
    j = np.arange((N_KBLK + 1) * TQ)
    rel_idx = np.clip(N_KBLK * TQ - j, -MAX_REL_DIST, MAX_REL_DIST) + MAX_REL_DIST
    return rel_bias.astype(F32)[:, rel_idx]


def kernel(x, norm_ffn1_g, ffn1_w_gate, ffn1_w_up, ffn1_w_down, norm_mix_g, w_in, rel_bias,
           w_out_att, w_out_ret, w_out, norm_ffn2_g, ffn2_w_gate, ffn2_w_up, ffn2_w_down,
           norm_final_g):
    batch, seq, d = x.shape
    depth = w_in.shape[0]
    assert depth >= 1 and (N_KBLK - 1) * TQ == N_PREV_CHUNKS * CHUNK and seq % TQ == 0
    cos2, sin2 = _rotary_tables(seq)
    xt = x.reshape(batch * seq, d)
    for l in range(depth):
        last = l == depth - 1
        g1 = norm_ffn1_g[l][None]
        y_lead, _, (wg1, wu1, wd1) = _ffn(xt, g1, ffn1_w_gate[l], ffn1_w_up[l], ffn1_w_down[l],
                                          n_tiles=1, tm=FFN_TM, tf=FFN_TF // 2)
        xt, (w_in_b, wg2, wu2, wd2), _ = _ffn(
            xt, g1, wg1, wu1, wd1, lead=y_lead, tm=FFN_TM, tf=FFN_TF,
            cast=((w_in[l], True), (ffn2_w_gate[l], True), (ffn2_w_up[l], True),
                  (ffn2_w_down[l], False)))
        proj, (woa, wor, wo) = _in_proj(
            xt, norm_mix_g[l][None], w_in_b, cos2, sin2, first_tile_scale=HEAD_DIM ** -0.5 * LOG2E,
            rot_tiles=((3, HEAD_DIM ** -0.5), (4, 1.0)),
            cast=((w_out_att[l], False), (w_out_ret[l], False), (w_out[l], False)))
        xt = _mixer_out(proj, xt, _rel_row(rel_bias[l]), woa, wor, wo, batch, seq)
        xt, _, _ = _ffn(xt, norm_ffn2_g[l][None], wg2, wu2, wd2, norm_final_g[None] if last else None,
                        tm=FFN_TM, tf=FFN_TF)
    return xt.reshape(batch, seq, d)
```

```python
import functools
import math

import numpy as np
import jax
import jax.numpy as jnp
from jax import lax
from jax.experimental import pallas as pl
from jax.experimental.pallas import tpu as pltpu

F32 = jnp.float32
BF16 = jnp.bfloat16

CHUNK = 64
CHUNK_SHIFT = CHUNK.bit_length() - 1
N_PREV_CHUNKS = 8
HEADS = 8
HEAD_DIM = 128
WIDTH = HEADS * HEAD_DIM
MAX_REL_DIST = 128
ROPE_BASE = 10000.0
EPS = 1e-6
NEG = -1e30
LOG2E = math.log2(math.e)

LANES = 128
BF16_SUBLANES = 16

TQ = 256
N_KBLK = 3
FFN_TM = 1024
FFN_TF = 512
FFN_PARTS = 2
W_RING = 3
MXU_WIDTH = 256
OUT_PIECE = MXU_WIDTH
MIB = 1024 * 1024
VMEM_LIMIT = 60 * MIB


def _rmsnorm(x, g):
    ms = jnp.mean(x * x, axis=-1, keepdims=True)
    return x * lax.rsqrt(ms + EPS) * g


def _silu(x):
    return x * jax.nn.sigmoid(x)


def _cast_plan(shape, n_i, n_j, blocked):
    r, c = shape
    same = lambda spec, every: (spec, spec, shape, every)
    if blocked:
        assert r % (n_i * BF16_SUBLANES) == 0 and c % (n_j * LANES) == 0, shape
        return (pl.BlockSpec((r // n_i, c // n_j), lambda i, j: (i, j)),
                pl.BlockSpec((None, r // n_i, c // n_j), lambda i, j: (j, i, 0)),
                (n_j, r, c // n_j), True)
    if r % (n_i * n_j * BF16_SUBLANES) == 0:
        return same(pl.BlockSpec((r // (n_i * n_j), c), lambda i, j: (i * n_j + j, 0)), True)
    if r % (n_i * BF16_SUBLANES) == 0 and c % (n_j * LANES) == 0:
        return same(pl.BlockSpec((r // n_i, c // n_j), lambda i, j: (i, j)), True)
    if r % (n_j * BF16_SUBLANES) == 0 and c % (n_i * LANES) == 0:
        return same(pl.BlockSpec((r // n_j, c // n_i), lambda i, j: (j, i)), True)
    if r % (n_i * BF16_SUBLANES) == 0:
        return same(pl.BlockSpec((r // n_i, c), lambda i, j: (i, 0)), False)
    rows = next(k for k in range(BF16_SUBLANES, r + 1, BF16_SUBLANES)
                if r % k == 0 and r // k <= n_i * n_j)
    last = r // rows - 1
    return same(pl.BlockSpec((rows, c), lambda i, j: (jnp.minimum(i * n_j + j, last), 0)), True)


def _cast_side_args(cast, n_i, n_j):
    plans = [_cast_plan(w.shape, n_i, n_j, blocked) for w, blocked in cast]
    order = ([k for k, p in enumerate(plans) if p[3]] + [k for k, p in enumerate(plans) if not p[3]])
    in_specs = [plans[k][0] for k in order]
    out_specs = [plans[k][1] for k in order]
    out_shapes = [jax.ShapeDtypeStruct(plans[k][2], BF16) for k in order]
    n_every = sum(1 for p in plans if p[3])
    return order, in_specs, out_specs, out_shapes, n_every


def _unpermute(values, order):
    out = [None] * len(order)
    for pos, k in enumerate(order):
        out[k] = values[pos]
    return out


def _cast_blocks(pairs):
    for src_ref, dst_ref in pairs:
        dst_ref[...] = src_ref[...].astype(BF16)


def _ffn_kernel(*refs, n_f, final_norm, own_cast, has_lead, n_cast, n_cast_every, n_parts):
    n_in = 6 if final_norm else 5
    n_own = 3 if own_cast else 0
    x_ref, g_ref = refs[:2]
    w_refs = refs[2:5]
    gf_ref = refs[5] if final_norm else None
    lead_ref = refs[n_in + n_cast] if has_lead else None
    n_inputs = n_in + n_cast + (1 if has_lead else 0)
    o_ref = refs[n_inputs]
    own_refs = refs[n_inputs + 1:n_inputs + 1 + n_own]
    casts = list(zip(refs[n_in:n_in + n_cast],
                     refs[n_inputs + 1 + n_own:n_inputs + 1 + n_own + n_cast]))
    cast_every, cast_first = casts[:n_cast_every], casts[n_cast_every:]
    h_ref = refs[n_inputs + 1 + n_own + n_cast]
    i = pl.program_id(0)
    j = pl.program_id(1)

    if own_cast:
        ring_refs = refs[n_inputs + 2 + n_own + n_cast:n_inputs + 5 + n_own + n_cast]
        ring_sem = refs[n_inputs + 5 + n_own + n_cast]
        tf = ring_refs[2].shape[1]

        def tile_copies(s):
            slot = s % W_RING if isinstance(s, int) else lax.rem(s, W_RING)
            cols = pl.ds(s * tf, tf)
            srcs = (w_refs[0].at[:, cols], w_refs[1].at[:, cols], w_refs[2].at[cols, :])
            return [pltpu.make_async_copy(src, ring.at[slot], ring_sem.at[k, slot])
                    for k, (src, ring) in enumerate(zip(srcs, ring_refs))]

        @pl.when(j == 0)
        def _():
            for s in range(W_RING - 1):
                for copy in tile_copies(s):
                    copy.start()

        @pl.when(j + (W_RING - 1) < n_f)
        def _():
            for copy in tile_copies(j + (W_RING - 1)):
                copy.start()

        for copy in tile_copies(j):
            copy.wait()
        f32_tiles = [ring.at[lax.rem(j, W_RING)] for ring in ring_refs]

    def partial_down(h, n_parts):
        if own_cast:
            _cast_blocks(zip(f32_tiles, own_refs))
        wg_ref, wu_ref, wd_ref = own_refs if own_cast else w_refs
        width = wd_ref.shape[0] // n_parts
        acts = []
        for p in range(n_parts):
            c = slice(p * width, (p + 1) * width)
            gate = jnp.dot(h, wg_ref[:, c], preferred_element_type=F32)
            up = jnp.dot(h, wu_ref[:, c], preferred_element_type=F32)
            acts.append((_silu(gate) * up).astype(BF16))
        downs = [jnp.dot(act, wd_ref[p * width:(p + 1) * width, :], preferred_element_type=F32)
                 for p, act in enumerate(acts)]
        return functools.reduce(jnp.add, downs)

    def when(cond):
        return pl.when(cond & (i > 0)) if has_lead else pl.when(cond)

    @when(j == 0)
    def _():
        _cast_blocks(cast_every + cast_first)
        h = _rmsnorm(x_ref[...], g_ref[...]).astype(BF16)
        h_ref[...] = h
        o_ref[...] = partial_down(h, n_parts)

    @when((j > 0) & (j < n_f - 1))
    def _():
        _cast_blocks(cast_every)
        o_ref[...] += partial_down(h_ref[...], n_parts)

    @when(j == n_f - 1)
    def _():
        _cast_blocks(cast_every)
        y = x_ref[...] + 0.5 * (o_ref[...] + partial_down(h_ref[...], 1))
        if final_norm:
            y = _rmsnorm(y, gf_ref[...])
        o_ref[...] = y

    if has_lead:
        @pl.when((i == 0) & (j == 0))
        def _():
            _cast_blocks(cast_every + cast_first)
            copy = pltpu.make_async_copy(lead_ref, o_ref, refs[-1])
            copy.start()
            copy.wait()

        @pl.when((i == 0) & (j > 0))
        def _():
            _cast_blocks(cast_every)


def _ffn(x, g, wg, wu, wd, g_final=None, *, tm, tf, cast=(), n_tiles=None, lead=None):
    t, d = x.shape
    n_i = t // tm if n_tiles is None else n_tiles
    n_f = wd.shape[0] // tf
    assert n_f >= 2
    final_norm = g_final is not None
    own_cast = wg.dtype == F32
    has_lead = lead is not None
    assert not (own_cast and has_lead)

    def fstep(i, j):
        return jnp.where(i == 0, 0, j) if has_lead else j

    def col_tile(w):
        if w.ndim == 3:
            return pl.BlockSpec((None, d, tf), lambda i, j: (fstep(i, j), 0, 0))
        return pl.BlockSpec((d, tf), lambda i, j: (0, fstep(i, j)))

    w_specs = [col_tile(wg), col_tile(wu), pl.BlockSpec((tf, d), lambda i, j: (fstep(i, j), 0))]
    row_spec = pl.BlockSpec((tm, d), lambda i, j: (i, 0))
    hbm = pl.BlockSpec(memory_space=pl.ANY)
    in_specs = ([row_spec, pl.BlockSpec((1, d), lambda i, j: (0, 0))]
                + ([hbm, hbm, hbm] if own_cast else w_specs))
    args = [x, g, wg, wu, wd]
    if final_norm:
        in_specs.append(pl.BlockSpec((1, d), lambda i, j: (0, 0)))
        args.append(g_final)
    order, cast_in, cast_out, cast_shapes, n_every = _cast_side_args(cast, n_i, n_f)
    in_specs += cast_in
    args += [cast[k][0] for k in order]
    scratch = [pltpu.VMEM((tm, d), BF16)]
    if own_cast:
        assert n_i == 1
        scratch += [pltpu.VMEM((W_RING, d, tf), F32), pltpu.VMEM((W_RING, d, tf), F32),
                    pltpu.VMEM((W_RING, tf, d), F32), pltpu.SemaphoreType.DMA((3, W_RING))]
    if has_lead:
        assert lead.shape == (tm, d)
        in_specs.append(pl.BlockSpec(memory_space=pl.ANY))
        args.append(lead)
        scratch.append(pltpu.SemaphoreType.DMA(()))
    own_specs = w_specs if own_cast else []
    own_shapes = [jax.ShapeDtypeStruct(w.shape, BF16) for w in (wg, wu, wd)] if own_cast else []
    outs = pl.pallas_call(
        functools.partial(_ffn_kernel, n_f=n_f, final_norm=final_norm, own_cast=own_cast,
                          has_lead=has_lead, n_cast=len(cast), n_cast_every=n_every,
                          n_parts=FFN_PARTS if tf % (FFN_PARTS * MXU_WIDTH) == 0 else 1),
        grid=(n_i, n_f),
        in_specs=in_specs,
        out_specs=[row_spec] + own_specs + cast_out,
        out_shape=[jax.ShapeDtypeStruct((n_i * tm, d), F32)] + own_shapes + cast_shapes,
        scratch_shapes=scratch,
        compiler_params=pltpu.CompilerParams(
            dimension_semantics=("arbitrary", "arbitrary"),
            vmem_limit_bytes=VMEM_LIMIT),
        name="ffn_final" if final_norm else ("ffn_lead" if own_cast else "ffn"),
    )(*args)
    n_own = len(own_shapes)
    return outs[0], _unpermute(outs[1 + n_own:], order), list(outs[1:1 + n_own])


def _in_proj_kernel(*refs, first_tile_scale, rot_tiles, n_cast, n_cast_every, n_i, n_j):
    x_hbm_ref, g_ref, w_hbm_ref, cos_ref, sin_ref = refs[:5]
    o_ref = refs[5 + n_cast]
    h_ref, wbuf_ref, wsem_ref, xbuf_ref, xsem_ref = refs[-5:]
    casts = list(zip(refs[5:5 + n_cast], refs[6 + n_cast:6 + 2 * n_cast]))
    cast_every, cast_first = casts[:n_cast_every], casts[n_cast_every:]
    i = pl.program_id(0)
    j = pl.program_id(1)
    step = i * n_j + j
    n_steps = n_i * n_j

    def rem(a, b):
        return a % b if isinstance(a, int) else lax.rem(a, b)

    def tile_copy(s):
        slot = rem(s, W_RING)
        return pltpu.make_async_copy(w_hbm_ref.at[rem(s, n_j)], wbuf_ref.at[slot], wsem_ref.at[slot])

    def x_copy(tile):
        slot = rem(tile, 2)
        return pltpu.make_async_copy(x_hbm_ref.at[tile], xbuf_ref.at[slot], xsem_ref.at[slot])

    @pl.when(step == 0)
    def _():
        x_copy(0).start()
        for s in range(W_RING - 1):
            tile_copy(s).start()

    @pl.when((j == 0) & (i + 1 < n_i))
    def _():
        x_copy(i + 1).start()

    @pl.when(step + (W_RING - 1) < n_steps)
    def _():
        tile_copy(step + (W_RING - 1)).start()

    tile_copy(step).wait()
    w_ref = wbuf_ref.at[rem(step, W_RING)]

    def project(h):
        return jnp.dot(h, w_ref[...], preferred_element_type=F32)

    @pl.when(j == 0)
    def _():
        _cast_blocks(cast_every + cast_first)
        x_copy(i).wait()
        h = _rmsnorm(xbuf_ref[rem(i, 2)], g_ref[...]).astype(BF16)
        h_ref[...] = h
        o_ref[...] = (project(h) * first_tile_scale).astype(BF16)

    plain = j > 0
    for tile, scale in rot_tiles:
        plain = plain & (j != tile)

        @pl.when(j == tile)
        def _(scale=scale):
            _cast_blocks(cast_every)
            acc = project(h_ref[...])
            cos = cos_ref[...]
            sin = sin_ref[...]
            for hd in range(o_ref.shape[1] // HEAD_DIM):
                cols = slice(hd * HEAD_DIM, (hd + 1) * HEAD_DIM)
                xh = acc[:, cols]
                rot = xh * cos + pltpu.roll(xh, HEAD_DIM // 2, axis=1) * sin
                o_ref[:, cols] = (rot * scale).astype(BF16)

    @pl.when(plain)
    def _():
        _cast_blocks(cast_every)
        o_ref[...] = project(h_ref[...]).astype(BF16)


def _in_proj(x, g, w, cos2, sin2, *, first_tile_scale, rot_tiles, cast=(), tm=1024, tn=WIDTH):
    t, d = x.shape
    n_j = w.shape[0]
    assert w.shape[1:] == (d, tn)
    n_i = t // tm
    seq_tiles = cos2.shape[0] // tm
    order, cast_in, cast_out, cast_shapes, n_every = _cast_side_args(cast, n_i, n_j)
    table = pl.BlockSpec((tm, HEAD_DIM), lambda i, j: (lax.rem(i, seq_tiles), 0))
    outs = pl.pallas_call(
        functools.partial(_in_proj_kernel, first_tile_scale=first_tile_scale, rot_tiles=rot_tiles,
                          n_cast=len(cast), n_cast_every=n_every, n_i=n_i, n_j=n_j),
        grid=(n_i, n_j),
        in_specs=[
            pl.BlockSpec(memory_space=pl.ANY),
            pl.BlockSpec((1, d), lambda i, j: (0, 0)),
            pl.BlockSpec(memory_space=pl.ANY),
            table, table,
        ] + cast_in,
        out_specs=[pl.BlockSpec((None, tm, tn), lambda i, j: (j, i, 0))] + cast_out,
        out_shape=[jax.ShapeDtypeStruct((n_j, t, tn), BF16)] + cast_shapes,
        scratch_shapes=[pltpu.VMEM((tm, d), BF16), pltpu.VMEM((W_RING, d, tn), BF16),
                        pltpu.SemaphoreType.DMA((W_RING,)),
                        pltpu.VMEM((2, tm, d), F32), pltpu.SemaphoreType.DMA((2,))],
        compiler_params=pltpu.CompilerParams(
            dimension_semantics=("arbitrary", "arbitrary"),
            vmem_limit_bytes=VMEM_LIMIT),
        name="in_proj",
    )(x.reshape(n_i, tm, d), g, w, cos2, sin2, *[cast[k][0] for k in order])
    return outs[0], _unpermute(outs[1:], order)


_LOG_GAMMA = tuple(math.log(1.0 - 2.0 ** (-5.0 - h)) for h in range(HEADS))


def _dot_nt(a, b):
    return lax.dot_general(a, b, (((1,), (1,)), ((), ())), preferred_element_type=F32)


def _lane_slabs(x):
    return [x[:, k:k + LANES] for k in range(0, x.shape[1], LANES)]


def _dot_tn(a, b):
    return lax.dot_general(a, b, (((0,), (0,)), ((), ())), preferred_element_type=F32)


def _mixer_out_kernel(qa_ref, ka0_ref, ka1_ref, ka2_ref, va0_ref, va1_ref, va2_ref,
                      qr_ref, kr_ref, vr_ref, gr_ref, ga0_ref, ga1_ref, gm0_ref, gm1_ref,
                      relrow_ref, xprev_ref, woa_ref, wor_ref, wo_ref,
                      o_ref,
                      bias_ref, decay_ref, state_ref,
                      att_ref, ret_ref, attp_ref, retp_ref, merged_ref, *, n_tiles, tiles_per_seq):
    s = pl.program_id(0)
    t = jnp.minimum(s, n_tiles - 1)
    i = lax.rem(t, tiles_per_seq)
    ka_refs = (ka0_ref, ka1_ref, ka2_ref)
    va_refs = (va0_ref, va1_ref, va2_ref)

    @pl.when(s == 0)
    def _():
        qc = lax.broadcasted_iota(jnp.int32, (TQ, N_KBLK * TQ), 0) >> CHUNK_SHIFT
        kc = lax.broadcasted_iota(jnp.int32, (TQ, N_KBLK * TQ), 1) >> CHUNK_SHIFT
        in_band = (kc >= qc) & (kc <= qc + N_PREV_CHUNKS)
        for h in range(HEADS):
            rows = jnp.broadcast_to(relrow_ref[h:h + 1, :], (TQ, (N_KBLK + 1) * TQ))
            toeplitz = pltpu.roll(rows, 0, axis=1, stride=1, stride_axis=0)
            bias_ref[h] = jnp.where(in_band, toeplitz[:, TQ:] * LOG2E, NEG)
        bias_ref[HEADS] = jnp.full((TQ, N_KBLK * TQ), NEG, F32)

        n = lax.broadcasted_iota(jnp.int32, (TQ, TQ), 0)
        m = lax.broadcasted_iota(jnp.int32, (TQ, TQ), 1)
        dist = jnp.abs(n - m).astype(F32)
        visible = (m >> CHUNK_SHIFT) <= (n >> CHUNK_SHIFT)
        for h in range(HEADS):
            decay_ref[h] = jnp.where(visible, jnp.exp(_LOG_GAMMA[h] * dist), 0.0)
        att_ref[...] = jnp.zeros_like(att_ref)
        ret_ref[...] = jnp.zeros_like(ret_ref)

    @pl.when(i == 0)
    def _():
        state_ref[...] = jnp.zeros_like(state_ref)

    attp_ref[...] = att_ref[...]
    retp_ref[...] = ret_ref[...]

    row = lax.broadcasted_iota(jnp.int32, (TQ, HEAD_DIM), 0).astype(F32)

    def scores(h):
        cols = slice(h * HEAD_DIM, (h + 1) * HEAD_DIM)
        q = qa_ref[:, cols]
        sc = []
        for j in range(N_KBLK):
            plane = h if j == N_KBLK - 1 else jnp.where(i + j >= N_KBLK - 1, h, HEADS)
            sc.append(_dot_nt(q, ka_refs[j][:, cols]) + bias_ref[plane, :, j * TQ:(j + 1) * TQ])
        return sc

    def attend(h, sc):
        cols = slice(h * HEAD_DIM, (h + 1) * HEAD_DIM)
        mx = jnp.maximum(jnp.maximum(jnp.max(sc[0], axis=-1, keepdims=True),
                                     jnp.max(sc[1], axis=-1, keepdims=True)),
                         jnp.max(sc[2], axis=-1, keepdims=True))
        acc = None
        den = None
        for j in range(N_KBLK):
            p = jnp.exp2(sc[j] - mx)
            dj = jnp.sum(p, axis=-1, keepdims=True)
            oj = jnp.dot(p.astype(BF16), va_refs[j][:, cols], preferred_element_type=F32)
            acc = oj if acc is None else acc + oj
            den = dj if den is None else den + dj
        att_ref[:, cols] = (acc / den).astype(BF16)

    def retain(h, fill):
        cols = slice(h * HEAD_DIM, (h + 1) * HEAD_DIM)
        qr = qr_ref[:, cols]
        kr = kr_ref[:, cols]
        kr_f = kr.astype(F32)
        v = vr_ref[:, cols]
        sr = _dot_nt(qr, kr)
        st = state_ref[h]
        lg = _LOG_GAMMA[h]
        cross = jnp.dot(qr, st.astype(BF16), preferred_element_type=F32)
        fill()
        inner = jnp.dot((sr * decay_ref[h]).astype(BF16), v, preferred_element_type=F32)
        kd = (kr_f * jnp.exp(lg * (TQ - 1.0 - row))).astype(BF16)
        state_ref[h] = st * math.exp(lg * TQ) + _dot_tn(kd, v)
        out = inner + cross * jnp.exp(lg * (row + 1.0))
        out = out * lax.rsqrt(jnp.mean(out * out, axis=-1, keepdims=True) + EPS)
        ret_ref[:, cols] = (_silu(gr_ref[:, cols].astype(F32)) * out).astype(BF16)

    gate_refs = ((ga0_ref, gm0_ref), (ga1_ref, gm1_ref))

    def merge_piece(k):
        cols = slice(k * OUT_PIECE, (k + 1) * OUT_PIECE)
        ga_ref, gm_ref = gate_refs[k * OUT_PIECE // WIDTH]
        gcols = slice(k * OUT_PIECE % WIDTH, k * OUT_PIECE % WIDTH + OUT_PIECE)
        branch_a = jnp.dot(attp_ref[...], woa_ref[:, cols], preferred_element_type=F32)
        branch_r = jnp.dot(retp_ref[...], wor_ref[:, cols], preferred_element_type=F32)
        merged = (jax.nn.sigmoid(ga_ref[:, gcols].astype(F32)) * branch_a
                  + jax.nn.sigmoid(gm_ref[:, gcols].astype(F32)) * branch_r)
        merged_ref[:, cols] = merged.astype(BF16)

    def out_piece(k):
        cols = slice(k * OUT_PIECE, (k + 1) * OUT_PIECE)
        o_ref[:, cols] = xprev_ref[:, cols] + jnp.dot(merged_ref[...], wo_ref[:, cols],
                                                      preferred_element_type=F32)

    n_pieces = merged_ref.shape[1] // OUT_PIECE
    pieces = ([functools.partial(merge_piece, k) for k in range(n_pieces)]
              + [functools.partial(out_piece, k) for k in range(n_pieces)])
    assert len(pieces) == 2 * HEADS
    sc = scores(0)
    for h in range(HEADS):
        sc_next = scores(h + 1) if h + 1 < HEADS else None
        pieces[2 * h]()
        attend(h, sc)
        retain(h, pieces[2 * h + 1])
        sc = sc_next


def _mixer_out(proj, x, rel_row, woa, wor, wo, batch, seq):
    t, d = x.shape
    tiles_per_seq = seq // TQ
    n_tiles = batch * tiles_per_seq
    proj4 = proj.reshape(proj.shape[0], batch, seq, WIDTH)

    def split(tt):
        return tt // tiles_per_seq, lax.rem(tt, tiles_per_seq)

    def cur(s):
        return split(jnp.minimum(s, n_tiles - 1))

    def tile(col):
        return pl.BlockSpec((None, None, TQ, WIDTH), lambda s: (col, *cur(s), 0))

    def prev_tile(col):
        return pl.BlockSpec((None, None, TQ, WIDTH),
                            lambda s: (col, *split(jnp.maximum(s - 1, 0)), 0))

    def hist(col, j):
        def index(s):
            b, i = cur(s)
            return col, b, jnp.maximum(i - (N_KBLK - 1) + j, 0), 0
        return pl.BlockSpec((None, None, TQ, WIDTH), index)

    def resident(shape):
        return pl.BlockSpec(shape, lambda s: (0, 0), pipeline_mode=pl.Buffered(1))

    prev = pl.BlockSpec((TQ, d), lambda s: (jnp.maximum(s - 1, 0), 0))
    gate0 = 7
    in_specs = ([tile(0)] + [hist(1, j) for j in range(N_KBLK)] + [hist(2, j) for j in range(N_KBLK)]
                + [tile(3), tile(4), tile(5), tile(6)]
                + [prev_tile(gate0 + c) for c in range(4)]
                + [resident(rel_row.shape), prev,
                   resident(woa.shape), resident(wor.shape), resident(wo.shape)])
    return pl.pallas_call(
        functools.partial(_mixer_out_kernel, n_tiles=n_tiles, tiles_per_seq=tiles_per_seq),
        grid=(n_tiles + 1,),
        in_specs=in_specs,
        out_specs=prev,
        out_shape=jax.ShapeDtypeStruct((t, d), F32),
        scratch_shapes=[pltpu.VMEM((HEADS + 1, TQ, N_KBLK * TQ), F32),
                        pltpu.VMEM((HEADS, TQ, TQ), F32),
                        pltpu.VMEM((HEADS, HEAD_DIM, HEAD_DIM), F32),
                        pltpu.VMEM((TQ, WIDTH), BF16),
                        pltpu.VMEM((TQ, WIDTH), BF16),
                        pltpu.VMEM((TQ, WIDTH), BF16),
                        pltpu.VMEM((TQ, WIDTH), BF16),
                        pltpu.VMEM((TQ, d), BF16)],
        compiler_params=pltpu.CompilerParams(
            dimension_semantics=("arbitrary",),
            vmem_limit_bytes=VMEM_LIMIT),
        name="mixer_out",
    )(*([proj4] * 15), rel_row, x, woa, wor, wo)


def _rotary_tables(seq):
    inv = 1.0 / (ROPE_BASE ** (np.arange(0, HEAD_DIM, 2, dtype=np.float64) / HEAD_DIM))
    ang = np.arange(seq, dtype=np.float64)[:, None] * inv[None, :]
    cos, sin = np.cos(ang), np.sin(ang)
    return (np.concatenate([cos, cos], axis=-1).astype(np.float32),
            np.concatenate([-sin, sin], axis=-1).astype(np.float32))


def _rel_row(rel_bias):
    j = np.arange((N_KBLK + 1) * TQ)
    rel_idx = np.clip(N_KBLK * TQ - j, -MAX_REL_DIST, MAX_REL_DIST) + MAX_REL_DIST
    return rel_bias.astype(F32)[:, rel_idx]


def kernel(x, norm_ffn1_g, ffn1_w_gate, ffn1_w_up, ffn1_w_down, norm_mix_g, w_in, rel_bias,
           w_out_att, w_out_ret, w_out, norm_ffn2_g, ffn2_w_gate, ffn2_w_up, ffn2_w_down,
           norm_final_g):
    batch, seq, d = x.shape
    depth = w_in.shape[0]
    assert depth >= 1 and (N_KBLK - 1) * TQ == N_PREV_CHUNKS * CHUNK and seq % TQ == 0
    cos2, sin2 = _rotary_tables(seq)
    xt = x.reshape(batch * seq, d)
    for l in range(depth):
        last = l == depth - 1
        g1 = norm_ffn1_g[l][None]
        y_lead, _, (wg1, wu1, wd1) = _ffn(xt, g1, ffn1_w_gate[l], ffn1_w_up[l], ffn1_w_down[l],
                                          n_tiles=1, tm=FFN_TM, tf=FFN_TF // 2)
        xt, (w_in_b, wg2, wu2, wd2), _ = _ffn(
            xt, g1, wg1, wu1, wd1, lead=y_lead, tm=FFN_TM, tf=FFN_TF,
            cast=((w_in[l], True), (ffn2_w_gate[l], True), (ffn2_w_up[l], True),
                  (ffn2_w_down[l], False)))
        proj, (woa, wor, wo) = _in_proj(
            xt, norm_mix_g[l][None], w_in_b, cos2, sin2, first_tile_scale=HEAD_DIM ** -0.5 * LOG2E,
            rot_tiles=((3, HEAD_DIM ** -0.5), (4, 1.0)),
            cast=((w_out_att[l], False), (w_out_ret[l], False), (w_out[l], False)))
        xt = _mixer_out(proj, xt, _rel_row(rel_bias[l]), woa, wor, wo, batch, seq)
        xt, _, _ = _ffn(xt, norm_ffn2_g[l][None], wg2, wu2, wd2, norm_final_g[None] if last else None,
                        tm=FFN_TM, tf=FFN_TF)
    return xt.reshape(batch, seq, d)
```
